```python
import jax, jax.numpy as jnp
from jax import lax
import numpy as np

D_MODEL = 1024
BATCH = 32
SEQ = 2048
DEPTH = 1
DEC_BATCH = 128
DEC_SEQ = 8
PAST_LEN = 8192
PAGE_SIZE = 128

D_LRU = D_MODEL // 2
LRU_BLOCKS = 8
LRU_BLOCK = D_LRU // LRU_BLOCKS
CONV_W = 4
LRU_C = 8.0
N_HEADS = 8
HEAD_DIM = 64
D_ATT = N_HEADS * HEAD_DIM
IDX_HEADS = 8
IDX_DIM = 64
TOPK_MAX = 256
Q_BLOCK = 128
ROPE_THETA = 10000.0
N_MEM = 256
C_HEADS = 4
C_HEAD_DIM = 128
D_CROSS = C_HEADS * C_HEAD_DIM
N_GROUPS = 4
EXPERTS_PER_GROUP = 8
N_EXPERTS = N_GROUPS * EXPERTS_PER_GROUP
EXPERT_TOPK = 2
D_EXPERT = D_MODEL // 2
MOE_BLOCK = 128
NORM_EPS = 1e-6
IN_SIZES = (D_LRU, D_ATT, D_ATT, D_ATT, IDX_HEADS * IDX_DIM, IDX_DIM, IDX_HEADS, D_MODEL, D_MODEL)
N_IN = D_LRU + 3 * D_ATT + IDX_HEADS * IDX_DIM + IDX_DIM + IDX_HEADS + 2 * D_MODEL

kernel_name = 'hawk_dsa_hmoe_step'

F32 = jnp.float32


def rmsnorm(x, g):
    xf = x.astype(F32)
    y = xf * lax.rsqrt(jnp.mean(xf * xf, axis=-1, keepdims=True) + NORM_EPS)
    return (y * g.astype(F32)).astype(x.dtype)


def rope(x, pos):
    d = x.shape[-1]
    inv = ROPE_THETA ** (-jnp.arange(0, d, 2, dtype=F32) / d)
    ang = pos.astype(F32)[:, None] * inv[None, :]
    cos = jnp.cos(ang)[:, None, :]
    sin = jnp.sin(ang)[:, None, :]
    xf = x.astype(F32)
    x1, x2 = xf[..., : d // 2], xf[..., d // 2:]
    return jnp.concatenate([x1 * cos - x2 * sin, x2 * cos + x1 * sin], axis=-1).astype(x.dtype)


def dsa_select(qi, wi, qpos, kidx, topk):
    s = jnp.einsum('thd,ld->thl', qi.astype(F32), kidx.astype(F32)) * (IDX_DIM ** -0.5)
    score = jnp.einsum('th,thl->tl', wi.astype(F32), jax.nn.relu(s))
    kpos = jnp.arange(kidx.shape[0])
    score = jnp.where(kpos[None, :] <= qpos[:, None], score, -jnp.inf)
    _, idx = lax.top_k(score, topk)
    return idx, idx <= qpos[:, None]


def sparse_attend(q, ks, vs, valid):
    logits = jnp.einsum('thd,tkhd->thk', q.astype(F32), ks.astype(F32)) * (HEAD_DIM ** -0.5)
    logits = jnp.where(valid[:, None, :], logits, -jnp.inf)
    p = jax.nn.softmax(logits, axis=-1)
    return jnp.einsum('thk,tkhd->thd', p, vs.astype(F32)).astype(q.dtype)


def attend_prompt(q, k, v, qi, ki, wi):
    T = q.shape[1]
    qb = min(Q_BLOCK, T)
    nqb = T // qb
    topk = min(TOPK_MAX, T // 4)

    def per_seq(args):
        q1, k1, v1, qi1, ki1, wi1 = args

        def per_block(j):
            start = j * qb
            qpos = start + jnp.arange(qb)
            sl = lambda a: lax.dynamic_slice_in_dim(a, start, qb, axis=0)
            idx, valid = dsa_select(sl(qi1), sl(wi1), qpos, ki1, topk)
            return sparse_attend(sl(q1), k1[idx], v1[idx], valid)

        return lax.map(per_block, jnp.arange(nqb)).reshape(T, N_HEADS, HEAD_DIM)

    return lax.map(per_seq, (q, k, v, qi, ki, wi))


def make_sample_attend(cache_k, cache_v, cache_kidx, page_table, layer):
    def attend(q, k, v, qi, ki, wi):
        T = q.shape[1]
        topk = min(TOPK_MAX, (PAST_LEN + T) // 4)
        qpos = PAST_LEN + jnp.arange(T)

        def per_seq(args):
            q1, k1, v1, qi1, ki1, wi1, pt = args
            ki_past = cache_kidx[layer, pt].reshape(PAST_LEN, IDX_DIM)
            kidx_all = jnp.concatenate([ki_past.astype(ki1.dtype), ki1], axis=0)
            idx, valid = dsa_select(qi1, wi1, qpos, kidx_all, topk)
            is_new = (idx >= PAST_LEN)[..., None, None]
            past = jnp.minimum(idx, PAST_LEN - 1)
            page = pt[past // PAGE_SIZE]
            off = past % PAGE_SIZE
            new = jnp.clip(idx - PAST_LEN, 0, T - 1)
            ks = jnp.where(is_new, k1[new], cache_k[layer, page, off].astype(k1.dtype))
            vs = jnp.where(is_new, v1[new], cache_v[layer, page, off].astype(v1.dtype))
            return sparse_attend(q1, ks, vs, valid)

        return lax.map(per_seq, (q, k, v, qi, ki, wi, page_table))
    return attend


def causal_conv(x, buf, w, b):
    T = x.shape[1]
    xp = jnp.concatenate([buf.astype(x.dtype), x], axis=1)
    y = xp[:, 0:T] * w[0]
    for j in range(1, CONV_W):
        y = y + xp[:, j:j + T] * w[j]
    return y + b, xp[:, -(CONV_W - 1):]


def rglru(x, h0, w_ra, b_ra, w_ri, b_ri, lam):
    B, T, _ = x.shape
    xf = x.astype(F32)
    xb = xf.reshape(B, T, LRU_BLOCKS, LRU_BLOCK)
    r = jax.nn.sigmoid(jnp.einsum('btnc,ncd->btnd', xb, w_ra.astype(F32)).reshape(B, T, D_LRU) + b_ra.astype(F32))
    i = jax.nn.sigmoid(jnp.einsum('btnc,ncd->btnd', xb, w_ri.astype(F32)).reshape(B, T, D_LRU) + b_ri.astype(F32))
    log_a = -LRU_C * r * jax.nn.softplus(-lam.astype(F32))
    a = jnp.exp(log_a)
    u = jnp.sqrt(-jnp.expm1(2.0 * log_a)) * (i * xf)

    def step(h, au):
        a_t, u_t = au
        h = a_t * h + u_t
        return h, h

    h_last, hs = lax.scan(step, h0.astype(F32), (jnp.swapaxes(a, 0, 1), jnp.swapaxes(u, 0, 1)))
    return jnp.swapaxes(hs, 0, 1).astype(x.dtype), h_last.astype(h0.dtype)


def mixer_block(u, pos, conv_buf, h0, attend, lw):
    B, T, _ = u.shape
    cuts = [int(c) for c in np.cumsum(IN_SIZES)[:-1]]
    xl, q, k, v, qi, ki, wi, g_lru, g_att = jnp.split(u @ lw['w_in'], cuts, axis=-1)
    q = rope(q.reshape(B, T, N_HEADS, HEAD_DIM), pos)
    k = rope(k.reshape(B, T, N_HEADS, HEAD_DIM), pos)
    v = v.reshape(B, T, N_HEADS, HEAD_DIM)
    qi = rope(qi.reshape(B, T, IDX_HEADS, IDX_DIM), pos)
    ki = rope(ki[:, :, None, :], pos)[:, :, 0, :]
    wi = wi * (IDX_HEADS ** -0.5)
    att = attend(q, k, v, qi, ki, wi)
    xc, new_buf = causal_conv(xl, conv_buf, lw['w_conv'], lw['b_conv'])
    hs, h_last = rglru(xc, h0, lw['w_ra'], lw['b_ra'], lw['w_ri'], lw['b_ri'], lw['lru_lambda'])
    mixed = (jax.nn.sigmoid(g_lru) * (hs @ lw['w_lru_out'])
             + jax.nn.sigmoid(g_att) * (att.reshape(B, T, D_ATT) @ lw['w_att_out']))
    return mixed @ lw['w_mix_out'], new_buf, h_last, k, v, ki


def mem_kv(mem, g_mem, w_mk, w_mv):
    B, S, _ = mem.shape
    m = rmsnorm(mem, g_mem)
    return (m @ w_mk).reshape(B, S, C_HEADS, C_HEAD_DIM), (m @ w_mv).reshape(B, S, C_HEADS, C_HEAD_DIM)


def cross_attend(u, mk, mv, w_cq, w_co):
    B, T, _ = u.shape
    q = (u @ w_cq).reshape(B, T, C_HEADS, C_HEAD_DIM)
    s = jnp.einsum('bthd,bshd->bhts', q.astype(F32), mk.astype(F32)) * (C_HEAD_DIM ** -0.5)
    p = jax.nn.softmax(s, axis=-1)
    o = jnp.einsum('bhts,bshd->bthd', p, mv.astype(F32)).astype(u.dtype)
    return o.reshape(B, T, D_CROSS) @ w_co


def moe_dispatch(xf, experts, gate, w_e1, w_e3, w_e2):
    N, D = xf.shape
    A = N * EXPERT_TOPK
    n_blocks = -(-(A + N_EXPERTS * (MOE_BLOCK - 1)) // MOE_BLOCK)
    P = n_blocks * MOE_BLOCK
    fe = experts.reshape(-1)
    ft = jnp.repeat(jnp.arange(N, dtype=jnp.int32), EXPERT_TOPK)
    fw = gate.reshape(-1)
    order = jnp.argsort(fe)
    se = fe[order]
    counts = jnp.bincount(fe, length=N_EXPERTS)
    padded = (counts + MOE_BLOCK - 1) // MOE_BLOCK * MOE_BLOCK
    pad_end = jnp.cumsum(padded)
    pad_start = pad_end - padded
    start = jnp.cumsum(counts) - counts
    dest = pad_start[se] + jnp.arange(A) - start[se]
    slot_tok = jnp.full((P,), N, jnp.int32).at[dest].set(ft[order])
    slot_w = jnp.zeros((P,), xf.dtype).at[dest].set(fw[order].astype(xf.dtype))
    block_exp = jnp.minimum(jnp.searchsorted(pad_end, jnp.arange(n_blocks) * MOE_BLOCK, side='right'), N_EXPERTS - 1)
    xpad = jnp.concatenate([xf, jnp.zeros((1, D), xf.dtype)], axis=0)
    xb = xpad[slot_tok].reshape(n_blocks, MOE_BLOCK, D)

    def run(args):
        xblk, e = args
        h = jax.nn.silu(xblk @ w_e1[e]) * (xblk @ w_e3[e])
        return h @ w_e2[e]

    yb = lax.map(run, (xb, block_exp)).reshape(P, D)
    y = jnp.zeros((N + 1, D), xf.dtype).at[slot_tok].add(yb * slot_w[:, None])
    return y[:N]


def hmoe(u, lw):
    B, T, D = u.shape
    N = B * T
    uf = u.reshape(N, D)
    rows = jnp.arange(N)
    lg = (uf @ lw['w_rg']).astype(F32) + lw['b_rg'].astype(F32)
    g_sel = jnp.argmax(lg, axis=-1)
    p_grp = jax.nn.softmax(lg, axis=-1)[rows, g_sel][:, None]
    le = ((uf @ lw['w_re']).astype(F32) + lw['b_re'].astype(F32)).reshape(N, N_GROUPS, EXPERTS_PER_GROUP)
    le = le[rows, g_sel]
    top_p, top_i = lax.top_k(jax.nn.softmax(le, axis=-1), EXPERT_TOPK)
    gate = p_grp * top_p / jnp.sum(top_p, axis=-1, keepdims=True)
    experts = (g_sel[:, None] * EXPERTS_PER_GROUP + top_i).astype(jnp.int32)
    return moe_dispatch(uf, experts, gate, lw['w_e1'], lw['w_e3'], lw['w_e2']).reshape(B, T, D)


def layer_forward(x, pos, conv_buf, h0, attend, mk, mv, lw):
    mix, buf, h_last, k, v, ki = mixer_block(rmsnorm(x, lw['g_mix']), pos, conv_buf, h0, attend, lw)
    x = x + mix
    x = x + cross_attend(rmsnorm(x, lw['g_cross']), mk, mv, lw['w_cq'], lw['w_co'])
    x = x + hmoe(rmsnorm(x, lw['g_ffn']), lw)
    return x, buf, h_last, k, v, ki


def setup_inputs(seed: int = 0) -> dict:
    key = jax.random.key(seed)
    ks = jax.random.split(key, 48)
    nrm = lambda k, shape, scale: jax.random.normal(k, shape, F32) * scale
    n_pages = PAST_LEN // PAGE_SIZE
    n_used = DEC_BATCH * n_pages
    n_pool = n_used + max(1, n_used // 4)
    a_c = jax.random.uniform(ks[9], (DEPTH, D_LRU), F32, 0.9, 0.999)
    a0 = a_c ** (1.0 / LRU_C)
    page_table = jax.random.permutation(ks[10], n_pool)[:n_used].reshape(DEC_BATCH, n_pages).astype(jnp.int32)
    return {
        'x_prompt': nrm(ks[0], (BATCH, SEQ, D_MODEL), 1.0),
        'mem_prompt': nrm(ks[1], (BATCH, N_MEM, D_MODEL), 1.0),
        'x_sample': nrm(ks[2], (DEC_BATCH, DEC_SEQ, D_MODEL), 1.0),
        'cache_k': nrm(ks[3], (DEPTH, n_pool, PAGE_SIZE, N_HEADS, HEAD_DIM), 1.0),
        'cache_v': nrm(ks[4], (DEPTH, n_pool, PAGE_SIZE, N_HEADS, HEAD_DIM), 1.0),
        'cache_kidx': nrm(ks[5], (DEPTH, n_pool, PAGE_SIZE, IDX_DIM), 1.0),
        'cache_mem_k': nrm(ks[6], (DEPTH, DEC_BATCH, N_MEM, C_HEADS, C_HEAD_DIM), 1.0),
        'cache_mem_v': nrm(ks[7], (DEPTH, DEC_BATCH, N_MEM, C_HEADS, C_HEAD_DIM), 1.0),
        'state_conv': nrm(ks[8], (DEPTH, DEC_BATCH, CONV_W - 1, D_LRU), 1.0),
        'state_lru': nrm(ks[11], (DEPTH, DEC_BATCH, D_LRU), 0.5),
        'page_table': page_table,
        'g_mix': 1.0 + nrm(ks[12], (DEPTH, D_MODEL), 0.02),
        'w_in': nrm(ks[13], (DEPTH, D_MODEL, N_IN), D_MODEL ** -0.5),
        'w_conv': nrm(ks[14], (DEPTH, CONV_W, D_LRU), CONV_W ** -0.5),
        'b_conv': nrm(ks[15], (DEPTH, D_LRU), 0.01),
        'w_ra': nrm(ks[16], (DEPTH, LRU_BLOCKS, LRU_BLOCK, LRU_BLOCK), LRU_BLOCK ** -0.5),
        'b_ra': nrm(ks[17], (DEPTH, D_LRU), 0.01),
        'w_ri': nrm(ks[18], (DEPTH, LRU_BLOCKS, LRU_BLOCK, LRU_BLOCK), LRU_BLOCK ** -0.5),
        'b_ri': nrm(ks[19], (DEPTH, D_LRU), 0.01),
        'lru_lambda': jnp.log(a0) - jnp.log1p(-a0),
        'w_lru_out': nrm(ks[20], (DEPTH, D_LRU, D_MODEL), D_LRU ** -0.5),
        'w_att_out': nrm(ks[21], (DEPTH, D_ATT, D_MODEL), D_ATT ** -0.5),
        'w_mix_out': nrm(ks[22], (DEPTH, D_MODEL, D_MODEL), D_MODEL ** -0.5),
        'g_cross': 1.0 + nrm(ks[23], (DEPTH, D_MODEL), 0.02),
        'g_mem': 1.0 + nrm(ks[24], (DEPTH, D_MODEL), 0.02),
        'w_cq': nrm(ks[25], (DEPTH, D_MODEL, D_CROSS), D_MODEL ** -0.5),
        'w_mk': nrm(ks[26], (DEPTH, D_MODEL, D_CROSS), D_MODEL ** -0.5),
        'w_mv': nrm(ks[27], (DEPTH, D_MODEL, D_CROSS), D_MODEL ** -0.5),
        'w_co': nrm(ks[28], (DEPTH, D_CROSS, D_MODEL), D_CROSS ** -0.5),
        'g_ffn': 1.0 + nrm(ks[29], (DEPTH, D_MODEL), 0.02),
        'w_rg': nrm(ks[30], (DEPTH, D_MODEL, N_GROUPS), D_MODEL ** -0.5),
        'b_rg': nrm(ks[31], (DEPTH, N_GROUPS), 0.01),
        'w_re': nrm(ks[32], (DEPTH, D_MODEL, N_EXPERTS), D_MODEL ** -0.5),
        'b_re': nrm(ks[33], (DEPTH, N_EXPERTS), 0.01),
        'w_e1': nrm(ks[34], (DEPTH, N_EXPERTS, D_MODEL, D_EXPERT), D_MODEL ** -0.5),
        'w_e3': nrm(ks[35], (DEPTH, N_EXPERTS, D_MODEL, D_EXPERT), D_MODEL ** -0.5),
        'w_e2': nrm(ks[36], (DEPTH, N_EXPERTS, D_EXPERT, D_MODEL), D_EXPERT ** -0.5),
        'g_final': 1.0 + nrm(ks[37], (D_MODEL,), 0.02),
    }


def reference(x_prompt, mem_prompt, x_sample, cache_k, cache_v, cache_kidx, cache_mem_k, cache_mem_v,
              state_conv, state_lru, page_table, g_mix, w_in, w_conv, b_conv, w_ra, b_ra, w_ri, b_ri,
              lru_lambda, w_lru_out, w_att_out, w_mix_out, g_cross, g_mem, w_cq, w_mk, w_mv, w_co,
              g_ffn, w_rg, b_rg, w_re, b_re, w_e1, w_e3, w_e2, g_final):
    B, T_p, _ = x_prompt.shape
    T_s = x_sample.shape[1]
    pos_p = jnp.arange(T_p)
    pos_s = PAST_LEN + jnp.arange(T_s)
    h_p, h_s = x_prompt, x_sample
    p_k, p_v, p_ki, p_cv, p_h, p_mk, p_mv = [], [], [], [], [], [], []
    s_k, s_v, s_ki, s_cv, s_h = [], [], [], [], []
    for l in range(DEPTH):
        lw = {
            'g_mix': g_mix[l], 'w_in': w_in[l], 'w_conv': w_conv[l], 'b_conv': b_conv[l],
            'w_ra': w_ra[l], 'b_ra': b_ra[l], 'w_ri': w_ri[l], 'b_ri': b_ri[l], 'lru_lambda': lru_lambda[l],
            'w_lru_out': w_lru_out[l], 'w_att_out': w_att_out[l], 'w_mix_out': w_mix_out[l],
            'g_cross': g_cross[l], 'w_cq': w_cq[l], 'w_co': w_co[l],
            'g_ffn': g_ffn[l], 'w_rg': w_rg[l], 'b_rg': b_rg[l], 'w_re': w_re[l], 'b_re': b_re[l],
            'w_e1': w_e1[l], 'w_e3': w_e3[l], 'w_e2': w_e2[l],
        }
        mk_p, mv_p = mem_kv(mem_prompt, g_mem[l], w_mk[l], w_mv[l])
        zbuf = jnp.zeros((B, CONV_W - 1, D_LRU), h_p.dtype)
        zh = jnp.zeros((B, D_LRU), h_p.dtype)
        h_p, buf, hl, k, v, ki = layer_forward(h_p, pos_p, zbuf, zh, attend_prompt, mk_p, mv_p, lw)
        p_k.append(k); p_v.append(v); p_ki.append(ki); p_cv.append(buf); p_h.append(hl)
        p_mk.append(mk_p); p_mv.append(mv_p)
        attend_s = make_sample_attend(cache_k, cache_v, cache_kidx, page_table, l)
        h_s, buf, hl, k, v, ki = layer_forward(h_s, pos_s, state_conv[l], state_lru[l], attend_s,
                                               cache_mem_k[l], cache_mem_v[l], lw)
        s_k.append(k); s_v.append(v); s_ki.append(ki); s_cv.append(buf); s_h.append(hl)
    y_prompt = rmsnorm(h_p, g_final)
    y_sample = rmsnorm(h_s, g_final)
    return (y_prompt, y_sample,
            jnp.stack(p_k), jnp.stack(p_v), jnp.stack(p_ki), jnp.stack(p_cv), jnp.stack(p_h),
            jnp.stack(p_mk), jnp.stack(p_mv),
            jnp.stack(s_k), jnp.stack(s_v), jnp.stack(s_ki), jnp.stack(s_cv), jnp.stack(s_h))
```

```python
import functools
import math

import jax
import jax.numpy as jnp
import numpy as np
from jax import lax
from jax.experimental import pallas as pl
from jax.experimental.pallas import tpu as pltpu

F32 = jnp.float32
BF16 = jnp.bfloat16
I32 = jnp.int32

N_HEADS = 8
HEAD_DIM = 64
IDX_HEADS = 8
IDX_DIM = 64
TOPK_MAX = 256
Q_BLOCK = 128
ROPE_THETA = 10000.0
CONV_W = 4
LRU_C = 8.0
LRU_BLOCKS = 8
C_HEADS = 4
N_GROUPS = 4
EXPERTS_PER_GROUP = 8
EXPERT_TOPK = 2
MOE_BLOCK = 128
NORM_EPS = 1e-6

LANES = 128
VMEM_LIMIT = 56 * 1024 * 1024


def _cparams(sem):
    return pltpu.CompilerParams(dimension_semantics=sem, vmem_limit_bytes=VMEM_LIMIT)


def _const_spec(shape):
    nd = len(shape)
    return pl.BlockSpec(shape, lambda *_: (0,) * nd)


def _rms(x, g):
    return x * lax.rsqrt(jnp.mean(x * x, axis=-1, keepdims=True) + NORM_EPS) * g


def _rope_rot(x, cos, sin_signed):
    w = x.shape[1]
    reps = w // LANES
    fwd = pltpu.roll(x, 32, axis=1)
    bwd = pltpu.roll(x, w - 32, axis=1)
    lane = lax.broadcasted_iota(I32, x.shape, 1)
    rot = jnp.where((lane % 64) < 32, bwd, fwd)
    if reps > 1:
        cos = jnp.concatenate([cos] * reps, axis=1)
        sin_signed = jnp.concatenate([sin_signed] * reps, axis=1)
    return x * cos + rot * sin_signed


def _in_proj_kernel(x_ref, g_ref, w_ref, cos_ref, sin_ref,
                    xl_ref, q_ref, k_ref, v_ref, qi_ref, kiwi_ref, gl_ref, ga_ref, *, d_lru, d_att, d_idx, d_model):
    u = _rms(x_ref[...], g_ref[...]).astype(BF16)
    cos = cos_ref[...]
    sin = sin_ref[...]

    def proj(c0, width):
        return jnp.dot(u, w_ref[:, c0:c0 + width], preferred_element_type=F32)

    c = 0
    xl_ref[...] = proj(c, d_lru); c += d_lru
    q_ref[...] = (_rope_rot(proj(c, d_att), cos, sin) * (HEAD_DIM ** -0.5)).astype(BF16); c += d_att
    k_ref[...] = _rope_rot(proj(c, d_att), cos, sin); c += d_att
    v_ref[...] = proj(c, d_att); c += d_att
    qi_ref[...] = (_rope_rot(proj(c, d_idx), cos, sin) * (IDX_DIM ** -0.5)).astype(BF16); c += d_idx
    kw = proj(c, LANES); c += LANES
    lane = lax.broadcasted_iota(I32, kw.shape, 1)
    kiwi_ref[...] = jnp.where(lane < IDX_DIM, _rope_rot(kw, cos, sin), kw * (IDX_HEADS ** -0.5))
    gl_ref[...] = jax.nn.sigmoid(proj(c, d_model)); c += d_model
    ga_ref[...] = jax.nn.sigmoid(proj(c, d_model))


def _rope_tables(pos, rows):
    inv = ROPE_THETA ** (-jnp.arange(0, HEAD_DIM, 2, dtype=F32) / HEAD_DIM)
    ang = pos.astype(F32)[:, None] * inv[None, :]
    cos = jnp.cos(ang)
    sin = jnp.sin(ang)
    cos_t = jnp.concatenate([cos, cos, cos, cos], axis=1)
    sin_t = jnp.concatenate([-sin, sin, -sin, sin], axis=1)
    reps = rows // pos.shape[0]
    if reps > 1:
        cos_t = jnp.tile(cos_t, (reps, 1))
        sin_t = jnp.tile(sin_t, (reps, 1))
    return cos_t, sin_t


def _in_proj(x2d, g, w_packed, pos, seq_len, d_lru, d_att, d_idx, tm):
    n, d_model = x2d.shape
    tm = min(tm, n)
    assert n % tm == 0
    rows = max(seq_len, tm)
    assert rows % tm == 0 and (tm % seq_len == 0 or seq_len % tm == 0)
    cos_t, sin_t = _rope_tables(pos, rows)
    nt = rows // tm
    row_spec = lambda w: pl.BlockSpec((tm, w), lambda i: (i, 0))
    tab_spec = pl.BlockSpec((tm, LANES), lambda i: (i % nt, 0))
    kern = functools.partial(_in_proj_kernel, d_lru=d_lru, d_att=d_att, d_idx=d_idx, d_model=d_model)
    out_shape = (
        jax.ShapeDtypeStruct((n, d_lru), F32),
        jax.ShapeDtypeStruct((n, d_att), BF16),
        jax.ShapeDtypeStruct((n, d_att), F32),
        jax.ShapeDtypeStruct((n, d_att), F32),
        jax.ShapeDtypeStruct((n, d_idx), BF16),
        jax.ShapeDtypeStruct((n, LANES), F32),
        jax.ShapeDtypeStruct((n, d_model), F32),
        jax.ShapeDtypeStruct((n, d_model), F32),
    )
    return pl.pallas_call(
        kern,
        grid=(n // tm,),
        in_specs=[row_spec(d_model), _const_spec((1, d_model)), _const_spec(w_packed.shape), tab_spec, tab_spec],
        out_specs=tuple(row_spec(s.shape[1]) for s in out_shape),
        out_shape=out_shape,
        compiler_params=_cparams(("parallel",)),
        name="in_proj",
    )(x2d, g.reshape(1, d_model), w_packed, cos_t, sin_t)


def _pack_w_in(w_in, d_lru, d_att, d_idx):
    d_model = w_in.shape[0]
    c = d_lru + 3 * d_att + d_idx
    kiwi = w_in[:, c:c + IDX_DIM + IDX_HEADS]
    pad = jnp.zeros((d_model, LANES - IDX_DIM - IDX_HEADS), w_in.dtype)
    return jnp.concatenate([w_in[:, :c], kiwi, pad, w_in[:, c + IDX_DIM + IDX_HEADS:]], axis=1).astype(BF16)


CONV_PAD = 8


def _softplus(x):
    return jnp.maximum(x, 0.0) + jnp.log1p(jnp.exp(-jnp.abs(x)))


def _rglru_kernel(xl_ref, buf0_ref, h0_ref, wconv_ref, bconv_ref, wra_ref, bra_ref, wri_ref, bri_ref, lam_ref,
                  hs_ref, hlast_ref, newbuf_ref, xpad_scr, a_scr, u_scr, h_scr, *, tc, bb):
    j = pl.program_id(1)
    d = xl_ref.shape[-1]
    hist = CONV_W - 1

    @pl.when(j == 0)
    def _():
        xpad_scr[:, 0:CONV_PAD - hist, :] = jnp.zeros((bb, CONV_PAD - hist, d), F32)
        xpad_scr[:, CONV_PAD - hist:CONV_PAD, :] = buf0_ref[...]
        h_scr[...] = h0_ref[...]

    xpad_scr[:, CONV_PAD:CONV_PAD + tc, :] = xl_ref[...]
    xc = jnp.zeros((bb, tc, d), F32) + bconv_ref[...]
    for i in range(CONV_W):
        off = CONV_PAD - hist + i
        xc = xc + xpad_scr[:, off:off + tc, :] * wconv_ref[i:i + 1, :]
    newbuf_ref[...] = xpad_scr[:, CONV_PAD + tc - hist:CONV_PAD + tc, :]
    xpad_scr[:, 0:CONV_PAD, :] = xpad_scr[:, tc:tc + CONV_PAD, :]

    xc2 = xc.reshape(bb * tc, d)
    xb = xc2.astype(BF16)
    r = jax.nn.sigmoid(jnp.dot(xb, wra_ref[...], preferred_element_type=F32) + bra_ref[...])
    g = jax.nn.sigmoid(jnp.dot(xb, wri_ref[...], preferred_element_type=F32) + bri_ref[...])
    log_a = (-LRU_C) * r * _softplus(-lam_ref[...])
    a = jnp.exp(log_a)
    mult = jnp.sqrt(-jnp.tanh(log_a) * (a * a + 1.0))
    a_scr[...] = a.reshape(bb, tc, d)
    u_scr[...] = (mult * (g * xc2)).reshape(bb, tc, d)

    def step(t, hs):
        new = []
        for b in range(bb):
            h = a_scr[b, pl.ds(t, 1), :] * hs[b] + u_scr[b, pl.ds(t, 1), :]
            u_scr[b, pl.ds(t, 1), :] = h
            new.append(h)
        return tuple(new)

    h_fin = lax.fori_loop(0, tc, step, tuple(h_scr[b] for b in range(bb)))
    for b in range(bb):
        h_scr[b] = h_fin[b]
        hlast_ref[b] = h_fin[b]
    hs_ref[...] = u_scr[...].astype(hs_ref.dtype)


def _block_diag(w):
    nb, c, _ = w.shape
    eye = jnp.eye(nb, dtype=w.dtype)
    return (eye[:, None, :, None] * w[:, :, None, :]).reshape(nb * c, nb * c)


def _rglru(xl, buf0, h0, w_conv, b_conv, w_ra, b_ra, w_ri, b_ri, lam, tc, bb):
    b, t, d = xl.shape
    tc = min(tc, t)
    bb = min(bb, b)
    assert t % tc == 0 and b % bb == 0 and tc % 8 == 0
    row = lambda a: a.reshape(1, d)
    kern = functools.partial(_rglru_kernel, tc=tc, bb=bb)
    hist = CONV_W - 1
    out_shape = (
        jax.ShapeDtypeStruct((b, t, d), BF16 if tc % 16 == 0 else F32),
        jax.ShapeDtypeStruct((b, 1, d), F32),
        jax.ShapeDtypeStruct((b, hist, d), F32),
    )
    hs, h_last, new_buf = pl.pallas_call(
        kern,
        grid=(b // bb, t // tc),
        in_specs=[
            pl.BlockSpec((bb, tc, d), lambda i, j: (i, j, 0)),
            pl.BlockSpec((bb, hist, d), lambda i, j: (i, 0, 0)),
            pl.BlockSpec((bb, 1, d), lambda i, j: (i, 0, 0)),
            _const_spec((CONV_W, d)), _const_spec((1, d)),
            _const_spec((d, d)), _const_spec((1, d)), _const_spec((d, d)), _const_spec((1, d)), _const_spec((1, d)),
        ],
        out_specs=(
            pl.BlockSpec((bb, tc, d), lambda i, j: (i, j, 0)),
            pl.BlockSpec((bb, 1, d), lambda i, j: (i, 0, 0)),
            pl.BlockSpec((bb, hist, d), lambda i, j: (i, 0, 0)),
        ),
        out_shape=out_shape,
        scratch_shapes=[
            pltpu.VMEM((bb, tc + CONV_PAD, d), F32),
            pltpu.VMEM((bb, tc, d), F32),
            pltpu.VMEM((bb, tc, d), F32),
            pltpu.VMEM((bb, 1, d), F32),
        ],
        compiler_params=_cparams(("parallel", "arbitrary")),
        name="rglru",
    )(xl, buf0, h0.reshape(b, 1, d), w_conv, row(b_conv),
      _block_diag(w_ra).astype(BF16), row(b_ra), _block_diag(w_ri).astype(BF16), row(b_ri), row(lam))
    return hs, h_last.reshape(b, d), new_buf


INT_MIN = -2 ** 31
NEG_INF_KEY = -2139095041


def _sortable_key(score):
    score = jnp.where(score == 0.0, 0.0, score)
    bits = pltpu.bitcast(score, I32)
    return bits ^ ((bits >> 31) & 0x7FFFFFFF)


def _pair_block_diag(x):
    lane = lax.broadcasted_iota(I32, x.shape, 1)
    zero = jnp.zeros_like(x)
    return jnp.concatenate([jnp.where(lane < 64, x, zero), jnp.where(lane >= 64, x, zero)], axis=0)


def _dot_nt(a, b):
    return lax.dot_general(a, b, (((1,), (1,)), ((), ())), preferred_element_type=F32)


def _selection_bias(kk, eq, tie_rank, theta, need):
    take = jnp.where(eq, tie_rank, jnp.inf) <= need
    b = jnp.where(take, 0.0, -jnp.inf)
    b = jnp.where(kk > theta, 0.0, b)
    return jnp.where(kk == NEG_INF_KEY, -jnp.inf, b)


def _prompt_attn_kernel(qi_ref, q_ref, kiwi_ref, k_ref, v_ref, att_ref,
                        kdup_scr, kbf_scr, vT_scr, qibd_scr, qbd_scr, key_scr, bias_scr, logit_scr, oT_scr,
                        *, topk, qb):
    j = pl.program_id(1)
    t_len = k_ref.shape[0]
    n_pairs = q_ref.shape[1] // LANES
    nk = j + 1

    def rows_of(c):
        return pl.ds(pl.multiple_of(c * qb, qb), qb)

    @pl.when(j == 0)
    def _():
        def prep(c, carry):
            rows = rows_of(c)
            kw = kiwi_ref[rows, :]
            lane = lax.broadcasted_iota(I32, kw.shape, 1)
            kdup_scr[rows, :] = jnp.where(lane < IDX_DIM, kw, pltpu.roll(kw, IDX_DIM, axis=1)).astype(BF16)
            kbf_scr[rows, :] = k_ref[rows, :].astype(BF16)
            vT_scr[:, rows] = v_ref[rows, :].T.astype(BF16)
            return carry
        lax.fori_loop(0, t_len // qb, prep, 0)

    for p in range(n_pairs):
        qibd_scr[p] = _pair_block_diag(qi_ref[:, p * LANES:(p + 1) * LANES])
        qbd_scr[p] = _pair_block_diag(q_ref[:, p * LANES:(p + 1) * LANES])
    w_t = kiwi_ref[rows_of(j), :].T

    row_i = lax.broadcasted_iota(I32, (qb, qb), 0)
    col_i = lax.broadcasted_iota(I32, (qb, qb), 1)

    def idx_chunk(c, carry):
        rows = rows_of(c)
        kd = kdup_scr[rows, :]
        sc = jnp.zeros((qb, qb), F32)
        for p in range(n_pairs):
            s2 = jnp.maximum(_dot_nt(kd, qibd_scr[p]), 0.0)
            h = IDX_DIM + 2 * p
            sc = sc + w_t[h:h + 1, :] * s2[:, :qb] + w_t[h + 1:h + 2, :] * s2[:, qb:]
        sc = jnp.where(c * qb + row_i <= j * qb + col_i, sc, -jnp.inf)
        key_scr[rows, :] = _sortable_key(sc)
        return carry
    lax.fori_loop(0, nk, idx_chunk, 0)

    def count(pred):
        def body(c, acc):
            hit = jnp.where(pred(key_scr[rows_of(c), :]), 1, 0).astype(I32)
            return acc + jnp.sum(hit.reshape(qb // 8, 8, qb), axis=0)
        acc = lax.fori_loop(0, nk, body, jnp.zeros((8, qb), I32))
        return jnp.sum(acc, axis=0, keepdims=True)

    def bit_step(i, prefix):
        cand_u = prefix | jnp.left_shift(jnp.int32(1), 31 - i)
        cand_s = cand_u ^ INT_MIN
        return jnp.where(count(lambda kk: kk >= cand_s) >= topk, cand_u, prefix)
    theta = lax.fori_loop(0, 32, bit_step, jnp.zeros((1, qb), I32)) ^ INT_MIN

    need = (topk - count(lambda kk: kk > theta)).astype(F32)
    tri = jnp.where(row_i >= col_i, 1.0, 0.0).astype(BF16)

    def mask_chunk(c, run):
        rows = rows_of(c)
        kk = key_scr[rows, :]
        eq = kk == theta
        pre = jnp.dot(tri, jnp.where(eq, 1.0, 0.0).astype(BF16), preferred_element_type=F32)
        bias_scr[rows, :] = _selection_bias(kk, eq, run + pre, theta, need)
        return run + pre[qb - 1:qb, :]
    lax.fori_loop(0, nk, mask_chunk, jnp.zeros((1, qb), F32))

    for p in range(n_pairs):
        lanes = slice(p * LANES, (p + 1) * LANES)
        qbd = qbd_scr[p]

        def pass1(c, m8):
            rows = rows_of(c)
            bias = bias_scr[rows, :]
            lg = _dot_nt(kbf_scr[rows, lanes], qbd) + jnp.concatenate([bias, bias], axis=1)
            logit_scr[rows, :] = lg
            return jnp.maximum(m8, jnp.max(lg.reshape(qb // 8, 8, 2 * qb), axis=0))
        m8 = lax.fori_loop(0, nk, pass1, jnp.full((8, 2 * qb), -jnp.inf, F32))
        m = jnp.max(m8, axis=0, keepdims=True)

        def pass2(c, carry):
            l8, acc = carry
            rows = rows_of(c)
            pe = jnp.exp(logit_scr[rows, :] - m)
            l8 = l8 + jnp.sum(pe.reshape(qb // 8, 8, 2 * qb), axis=0)
            acc = acc + jnp.dot(vT_scr[lanes, rows], pe.astype(BF16), preferred_element_type=F32)
            return l8, acc
        l8, acc = lax.fori_loop(0, nk, pass2, (jnp.zeros((8, 2 * qb), F32), jnp.zeros((LANES, 2 * qb), F32)))
        o = acc / jnp.sum(l8, axis=0, keepdims=True)
        oT_scr[p * LANES:p * LANES + HEAD_DIM, :] = o[:HEAD_DIM, :qb]
        oT_scr[p * LANES + HEAD_DIM:(p + 1) * LANES, :] = o[HEAD_DIM:, qb:]
    att_ref[...] = oT_scr[...].T.astype(att_ref.dtype)


def _prompt_attention(qi, q, kiwi, k, v, batch, t_len):
    n, d_att = q.shape
    qb = min(Q_BLOCK, t_len)
    assert qb == LANES and t_len % qb == 0
    nqb = t_len // qb
    topk = min(TOPK_MAX, t_len // 4)
    n_pairs = d_att // LANES
    kern = functools.partial(_prompt_attn_kernel, topk=topk, qb=qb)
    blk_spec = lambda w: pl.BlockSpec((qb, w), lambda b, j: (b * nqb + j, 0))
    seq_spec = lambda w: pl.BlockSpec((t_len, w), lambda b, j: (b, 0))
    return pl.pallas_call(
        kern,
        grid=(batch, nqb),
        in_specs=[blk_spec(qi.shape[1]), blk_spec(d_att), seq_spec(LANES), seq_spec(d_att), seq_spec(d_att)],
        out_specs=blk_spec(d_att),
        out_shape=jax.ShapeDtypeStruct((n, d_att), BF16),
        scratch_shapes=[
            pltpu.VMEM((t_len, LANES), BF16),
            pltpu.VMEM((t_len, d_att), BF16),
            pltpu.VMEM((d_att, t_len), BF16),
            pltpu.VMEM((n_pairs, 2 * qb, LANES), BF16),
            pltpu.VMEM((n_pairs, 2 * qb, LANES), BF16),
            pltpu.VMEM((t_len, qb), I32),
            pltpu.VMEM((t_len, qb), F32),
            pltpu.VMEM((t_len, 2 * qb), F32),
            pltpu.VMEM((d_att, qb), F32),
        ],
        compiler_params=_cparams(("parallel", "arbitrary")),
        name="prompt_attn",
    )(qi, q, kiwi, k, v)


def _sample_select_kernel(pt_ref, qi_ref, wi_ref, kinew_ref, *rest, topk, pg, past, page, t_new):
    del pt_ref
    page_refs = rest[:pg]
    bias_ref = rest[pg]
    key_scr = rest[pg + 1]
    g = pl.program_id(1)
    n_chunks = past // page + 1
    qi = qi_ref[...]
    wi = wi_ref[...]

    def scores(kb):
        s = jnp.maximum(_dot_nt(qi, kb), 0.0) * wi
        return jnp.sum(s.reshape(IDX_HEADS, t_new, kb.shape[0]), axis=0)

    for i in range(pg):
        off = pl.multiple_of((g * pg + i) * page, page)
        key_scr[:, pl.ds(off, page)] = _sortable_key(scores(page_refs[i][0, 0].astype(BF16)))

    @pl.when(g == pl.num_programs(1) - 1)
    def _():
        knew = jnp.concatenate([kinew_ref[:, :IDX_DIM], jnp.zeros((page - t_new, IDX_DIM), F32)], axis=0)
        sc = scores(knew.astype(BF16))
        tok = lax.broadcasted_iota(I32, sc.shape, 0)
        kk_i = lax.broadcasted_iota(I32, sc.shape, 1)
        key_scr[:, past:past + page] = _sortable_key(jnp.where(kk_i <= tok, sc, -jnp.inf))

        def chunk(c):
            return key_scr[:, c * page:(c + 1) * page]

        def count(pred):
            acc = jnp.zeros((t_new, page), I32)
            for c in range(n_chunks):
                acc = acc + jnp.where(pred(chunk(c)), 1, 0).astype(I32)
            return jnp.sum(acc, axis=1, keepdims=True)

        def bit_step(i, prefix):
            cand_u = prefix | jnp.left_shift(jnp.int32(1), 31 - i)
            cand_s = cand_u ^ INT_MIN
            return jnp.where(count(lambda kk: kk >= cand_s) >= topk, cand_u, prefix)
        theta = lax.fori_loop(0, 32, bit_step, jnp.zeros((t_new, 1), I32)) ^ INT_MIN
        need = (topk - count(lambda kk: kk > theta)).astype(F32)

        row_i = lax.broadcasted_iota(I32, (page, page), 0)
        col_i = lax.broadcasted_iota(I32, (page, page), 1)
        triu = jnp.where(row_i <= col_i, 1.0, 0.0).astype(BF16)
        run = jnp.zeros((t_new, 1), F32)
        for c in range(n_chunks):
            kk = chunk(c)
            eq = kk == theta
            pre = jnp.dot(jnp.where(eq, 1.0, 0.0).astype(BF16), triu, preferred_element_type=F32)
            bias_ref[0, :, c * page:(c + 1) * page] = _selection_bias(kk, eq, run + pre, theta, need)
            run = run + pre[:, page - 1:page]


def _sample_attn_kernel(pt_ref, q_ref, bias_ref, knew_ref, vnew_ref, *rest, pg, past, page, t_new):
    del pt_ref
    k_refs = rest[:pg]
    v_refs = rest[pg:2 * pg]
    out_ref = rest[2 * pg]
    m_scr, l_scr, acc_scr = rest[2 * pg + 1:]
    g = pl.program_id(1)
    q = q_ref[...]

    @pl.when(g == 0)
    def _():
        m_scr[...] = jnp.full(m_scr.shape, -jnp.inf, F32)
        l_scr[...] = jnp.zeros(l_scr.shape, F32)
        acc_scr[...] = jnp.zeros(acc_scr.shape, F32)

    def attend(k_heads, v_heads, bias):
        lg = jnp.concatenate(
            [_dot_nt(q[h * t_new:(h + 1) * t_new, :].astype(BF16), k_heads[h]) for h in range(N_HEADS)], axis=0)
        lg = lg + jnp.concatenate([bias] * N_HEADS, axis=0)
        m_old = m_scr[...]
        m_new = jnp.maximum(m_old, jnp.max(lg, axis=1, keepdims=True))
        m_safe = jnp.where(m_new == -jnp.inf, 0.0, m_new)
        alpha = jnp.exp(m_old - m_safe)
        pe = jnp.exp(lg - m_safe)
        l_scr[...] = alpha * l_scr[...] + jnp.sum(pe, axis=1, keepdims=True)
        pv = jnp.concatenate(
            [jnp.dot(pe[h * t_new:(h + 1) * t_new, :].astype(BF16), v_heads[h], preferred_element_type=F32)
             for h in range(N_HEADS)], axis=0)
        acc_scr[...] = alpha * acc_scr[...] + pv
        m_scr[...] = m_new

    def head_slabs(refs, h):
        return jnp.concatenate([r[0, 0, :, h, :] for r in refs], axis=0).astype(BF16)

    off = pl.multiple_of(g * (pg * page), page)
    attend([head_slabs(k_refs, h) for h in range(N_HEADS)],
           [head_slabs(v_refs, h) for h in range(N_HEADS)],
           bias_ref[0, :, pl.ds(off, pg * page)])

    @pl.when(g == pl.num_programs(1) - 1)
    def _():
        def new_slabs(ref):
            pad = jnp.zeros((page - t_new, HEAD_DIM), F32)
            return [jnp.concatenate([ref[:, h * HEAD_DIM:(h + 1) * HEAD_DIM], pad], axis=0).astype(BF16)
                    for h in range(N_HEADS)]
        attend(new_slabs(knew_ref), new_slabs(vnew_ref), bias_ref[0, :, past:past + page])
        out_ref[0] = acc_scr[...] / l_scr[...]


def _to_head_major(x, batch, t_new, heads, dim):
    return x.reshape(batch, t_new, heads, dim).transpose(0, 2, 1, 3).reshape(batch * heads * t_new, dim)


def _sample_attention(qi, q, kiwi, k_new, v_new, cache_k, cache_v, cache_kidx, page_table, layer, t_new, pg):
    batch, n_pages = page_table.shape
    page = cache_k.shape[2]
    past = n_pages * page
    pg = min(pg, n_pages)
    assert n_pages % pg == 0 and page == LANES and t_new == 8
    n_steps = n_pages // pg
    l_pad = past + page
    topk = min(TOPK_MAX, (past + t_new) // 4)
    rows = IDX_HEADS * t_new

    qi_hm = _to_head_major(qi, batch, t_new, IDX_HEADS, IDX_DIM)
    wi_col = kiwi[:, IDX_DIM:IDX_DIM + IDX_HEADS].reshape(batch, t_new, IDX_HEADS).transpose(0, 2, 1).reshape(-1, 1)
    seq_spec = lambda r, w: pl.BlockSpec((r, w), lambda b, g, pt: (b, 0))
    bias_spec = pl.BlockSpec((1, t_new, l_pad), lambda b, g, pt: (b, 0, 0))

    def page_spec(arr, i):
        blk = (1, 1) + arr.shape[2:]
        zeros = (0,) * (arr.ndim - 2)
        return pl.BlockSpec(blk, lambda b, g, pt: (layer, pt[b, g * pg + i]) + zeros)

    bias = pl.pallas_call(
        functools.partial(_sample_select_kernel, topk=topk, pg=pg, past=past, page=page, t_new=t_new),
        grid_spec=pltpu.PrefetchScalarGridSpec(
            num_scalar_prefetch=1,
            grid=(batch, n_steps),
            in_specs=[seq_spec(rows, IDX_DIM), seq_spec(rows, 1), seq_spec(t_new, LANES)]
                     + [page_spec(cache_kidx, i) for i in range(pg)],
            out_specs=bias_spec,
            scratch_shapes=[pltpu.VMEM((t_new, l_pad), I32)],
        ),
        out_shape=jax.ShapeDtypeStruct((batch, t_new, l_pad), F32),
        compiler_params=_cparams(("parallel", "arbitrary")),
        name="sample_select",
    )(page_table, qi_hm, wi_col, kiwi, *([cache_kidx] * pg))

    q_hm = _to_head_major(q, batch, t_new, N_HEADS, HEAD_DIM).astype(F32)
    out = pl.pallas_call(
        functools.partial(_sample_attn_kernel, pg=pg, past=past, page=page, t_new=t_new),
        grid_spec=pltpu.PrefetchScalarGridSpec(
            num_scalar_prefetch=1,
            grid=(batch, n_steps),
            in_specs=[seq_spec(rows, HEAD_DIM), bias_spec, seq_spec(t_new, k_new.shape[1]), seq_spec(t_new, v_new.shape[1])]
                     + [page_spec(cache_k, i) for i in range(pg)] + [page_spec(cache_v, i) for i in range(pg)],
            out_specs=pl.BlockSpec((1, rows, HEAD_DIM), lambda b, g, pt: (b, 0, 0)),
            scratch_shapes=[pltpu.VMEM((rows, 1), F32), pltpu.VMEM((rows, 1), F32), pltpu.VMEM((rows, HEAD_DIM), F32)],
        ),
        out_shape=jax.ShapeDtypeStruct((batch, rows, HEAD_DIM), F32),
        compiler_params=_cparams(("parallel", "arbitrary")),
        name="sample_attn",
    )(page_table, q_hm, bias, k_new, v_new, *([cache_k] * pg), *([cache_v] * pg))
    att = out.reshape(batch, N_HEADS, t_new, HEAD_DIM).transpose(0, 2, 1, 3).reshape(batch * t_new, N_HEADS * HEAD_DIM)
    return att.astype(BF16)


def _mix_kernel(x_ref, hs_ref, att_ref, gl_ref, ga_ref, wl_ref, wa_ref, wm_ref, gc_ref, wcq_ref, x1_ref, qc_ref):
    lru = jnp.dot(hs_ref[...].astype(BF16), wl_ref[...], preferred_element_type=F32)
    att = jnp.dot(att_ref[...], wa_ref[...], preferred_element_type=F32)
    mixed = gl_ref[...] * lru + ga_ref[...] * att
    x1 = x_ref[...] + jnp.dot(mixed.astype(BF16), wm_ref[...], preferred_element_type=F32)
    x1_ref[...] = x1
    qc_ref[...] = jnp.dot(_rms(x1, gc_ref[...]).astype(BF16), wcq_ref[...], preferred_element_type=F32)


def _mix(x2d, hs, att, gl, ga, w_lru_out, w_att_out, w_mix_out, g_cross, w_cq, tm):
    n, d_model = x2d.shape
    tm = min(tm, n)
    assert n % tm == 0
    d_cross = w_cq.shape[1]
    row_spec = lambda w: pl.BlockSpec((tm, w), lambda i: (i, 0))
    return pl.pallas_call(
        _mix_kernel,
        grid=(n // tm,),
        in_specs=[row_spec(d_model), row_spec(hs.shape[1]), row_spec(att.shape[1]), row_spec(d_model), row_spec(d_model),
                  _const_spec(w_lru_out.shape), _const_spec(w_att_out.shape), _const_spec(w_mix_out.shape),
                  _const_spec((1, d_model)), _const_spec(w_cq.shape)],
        out_specs=(row_spec(d_model), row_spec(d_cross)),
        out_shape=(jax.ShapeDtypeStruct((n, d_model), F32), jax.ShapeDtypeStruct((n, d_cross), F32)),
        compiler_params=_cparams(("parallel",)),
        name="mix_out",
    )(x2d, hs, att, gl, ga, w_lru_out.astype(BF16), w_att_out.astype(BF16), w_mix_out.astype(BF16),
      g_cross.reshape(1, d_model), w_cq.astype(BF16))


def _mem_kv_kernel(mem_ref, g_ref, wk_ref, wv_ref, mk_ref, mv_ref):
    m = _rms(mem_ref[0], g_ref[...]).astype(BF16)
    mk = jnp.dot(m, wk_ref[...], preferred_element_type=F32)
    mv = jnp.dot(m, wv_ref[...], preferred_element_type=F32)
    hd = mk_ref.shape[-1]
    for h in range(C_HEADS):
        mk_ref[0, :, h, :] = mk[:, h * hd:(h + 1) * hd]
        mv_ref[0, :, h, :] = mv[:, h * hd:(h + 1) * hd]


def _mem_kv(mem, g_mem, w_mk, w_mv):
    b, s, d_model = mem.shape
    hd = w_mk.shape[1] // C_HEADS
    kv_spec = pl.BlockSpec((1, s, C_HEADS, hd), lambda i: (i, 0, 0, 0))
    kv_shape = jax.ShapeDtypeStruct((b, s, C_HEADS, hd), F32)
    return pl.pallas_call(
        _mem_kv_kernel,
        grid=(b,),
        in_specs=[pl.BlockSpec((1, s, d_model), lambda i: (i, 0, 0)), _const_spec((1, d_model)),
                  _const_spec(w_mk.shape), _const_spec(w_mv.shape)],
        out_specs=(kv_spec, kv_spec),
        out_shape=(kv_shape, kv_shape),
        compiler_params=_cparams(("parallel",)),
        name="mem_kv",
    )(mem, g_mem.reshape(1, d_model), w_mk.astype(BF16), w_mv.astype(BF16))


ROUTE_EXPERT0, ROUTE_EXPERT1, ROUTE_GATE0, ROUTE_GATE1 = 0, 1, 2, 3


def _lane_min_where(cond, lane):
    return jnp.min(jnp.where(cond, lane, LANES), axis=1, keepdims=True)


def _cross_kernel(x1_ref, qc_ref, mk_ref, mv_ref, wco_ref, gf_ref, wr_ref, br_ref, x2_ref, u3_ref, route_ref):
    hd = mk_ref.shape[-1]
    scale = hd ** -0.5
    outs = []
    for h in range(C_HEADS):
        mk = mk_ref[0, :, h, :].astype(BF16)
        mv = mv_ref[0, :, h, :].astype(BF16)
        s = _dot_nt(qc_ref[:, h * hd:(h + 1) * hd].astype(BF16), mk) * scale
        pe = jnp.exp(s - jnp.max(s, axis=1, keepdims=True))
        o = jnp.dot(pe.astype(BF16), mv, preferred_element_type=F32)
        outs.append(o / jnp.sum(pe, axis=1, keepdims=True))
    o = jnp.concatenate(outs, axis=1).astype(BF16)
    x2 = x1_ref[...] + jnp.dot(o, wco_ref[...], preferred_element_type=F32)
    x2_ref[...] = x2
    u3 = _rms(x2, gf_ref[...])
    u3_ref[...] = u3

    logits = jnp.dot(u3.astype(BF16), wr_ref[...], preferred_element_type=F32) + br_ref[...]
    lane = lax.broadcasted_iota(I32, logits.shape, 1)
    lg = jnp.where(lane < N_GROUPS, logits, -jnp.inf)
    g_max = jnp.max(lg, axis=1, keepdims=True)
    g_sel = _lane_min_where(lg == g_max, lane)
    p_grp = 1.0 / jnp.sum(jnp.exp(lg - g_max), axis=1, keepdims=True)
    e_lo = N_GROUPS + g_sel * EXPERTS_PER_GROUP
    le = jnp.where((lane >= e_lo) & (lane < e_lo + EXPERTS_PER_GROUP), logits, -jnp.inf)
    m1 = jnp.max(le, axis=1, keepdims=True)
    i1 = _lane_min_where(le == m1, lane)
    le2 = jnp.where(lane == i1, -jnp.inf, le)
    m2 = jnp.max(le2, axis=1, keepdims=True)
    i2 = _lane_min_where(le2 == m2, lane)
    e2 = jnp.exp(m2 - m1)
    inv = p_grp / (1.0 + e2)
    route = jnp.where(lane == ROUTE_EXPERT0, (i1 - N_GROUPS).astype(F32), 0.0)
    route = jnp.where(lane == ROUTE_EXPERT1, (i2 - N_GROUPS).astype(F32), route)
    route = jnp.where(lane == ROUTE_GATE0, inv, route)
    route_ref[...] = jnp.where(lane == ROUTE_GATE1, inv * e2, route)


def _pack_router(w_rg, b_rg, w_re, b_re):
    d_model = w_rg.shape[0]
    used = w_rg.shape[1] + w_re.shape[1]
    w = jnp.concatenate([w_rg, w_re, jnp.zeros((d_model, LANES - used), w_rg.dtype)], axis=1).astype(BF16)
    b = jnp.concatenate([b_rg, b_re, jnp.zeros((LANES - used,), b_rg.dtype)]).reshape(1, LANES)
    return w, b


def _cross(x1, qc, mk, mv, w_co, g_ffn, w_router, b_router, batch, t_len, tq):
    n, d_model = x1.shape
    tq = min(tq, t_len)
    assert t_len % tq == 0
    nq = t_len // tq
    d_cross = qc.shape[1]
    row_spec = lambda w: pl.BlockSpec((tq, w), lambda b, j: (b * nq + j, 0))
    mem_spec = pl.BlockSpec((1,) + mk.shape[1:], lambda b, j: (b, 0, 0, 0))
    return pl.pallas_call(
        _cross_kernel,
        grid=(batch, nq),
        in_specs=[row_spec(d_model), row_spec(d_cross), mem_spec, mem_spec, _const_spec(w_co.shape),
                  _const_spec((1, d_model)), _const_spec(w_router.shape), _const_spec((1, LANES))],
        out_specs=(row_spec(d_model), row_spec(d_model), row_spec(LANES)),
        out_shape=(jax.ShapeDtypeStruct((n, d_model), F32), jax.ShapeDtypeStruct((n, d_model), F32),
                   jax.ShapeDtypeStruct((n, LANES), F32)),
        compiler_params=_cparams(("parallel", "parallel")),
        name="cross_router",
    )(x1, qc, mk, mv, w_co.astype(BF16), g_ffn.reshape(1, d_model), w_router, b_router)


def _row_gather_start(src_hbm, idx_ref, dst, sem, n_rows):
    def issue(r, carry):
        pltpu.make_async_copy(src_hbm.at[pl.ds(idx_ref[0, 0, r], 1)], dst.at[pl.ds(r, 1)], sem).start()
        return carry
    lax.fori_loop(0, n_rows, issue, 0)


def _row_gather_wait(src_hbm, dst, sem, n_rows):
    pltpu.make_async_copy(src_hbm.at[pl.ds(0, n_rows)], dst, sem).wait()


def _expert_kernel(be_ref, tok_ref, tok_next_ref, x_hbm, w1_ref, w3_ref, w2_ref, y_ref, xbuf, sems):
    del be_ref
    i = pl.program_id(0)
    nb = pl.num_programs(0)
    rows = xbuf.shape[1]
    slot = i % 2

    @pl.when(i == 0)
    def _():
        _row_gather_start(x_hbm, tok_ref, xbuf.at[0], sems.at[0], rows)

    @pl.when(i + 1 < nb)
    def _():
        _row_gather_start(x_hbm, tok_next_ref, xbuf.at[1 - slot], sems.at[1 - slot], rows)

    _row_gather_wait(x_hbm, xbuf.at[slot], sems.at[slot], rows)
    x = xbuf[slot].astype(BF16)
    a = jnp.dot(x, w1_ref[0], preferred_element_type=F32)
    b = jnp.dot(x, w3_ref[0], preferred_element_type=F32)
    h = (a * jax.nn.sigmoid(a) * b).astype(BF16)
    y_ref[...] = jnp.dot(h, w2_ref[0], preferred_element_type=F32)


def _combine_kernel(s0_ref, s1_ref, s0n_ref, s1n_ref, yb_hbm, x2_ref, route_ref, gfin_ref, out_ref, ybuf, sems):
    i = pl.program_id(0)
    nb = pl.num_programs(0)
    rows = x2_ref.shape[0]
    slot = i % 2

    def start(a_ref, b_ref, s):
        _row_gather_start(yb_hbm, a_ref, ybuf.at[s, 0], sems.at[s, 0], rows)
        _row_gather_start(yb_hbm, b_ref, ybuf.at[s, 1], sems.at[s, 1], rows)

    @pl.when(i == 0)
    def _():
        start(s0_ref, s1_ref, 0)

    @pl.when(i + 1 < nb)
    def _():
        start(s0n_ref, s1n_ref, 1 - slot)

    _row_gather_wait(yb_hbm, ybuf.at[slot, 0], sems.at[slot, 0], rows)
    _row_gather_wait(yb_hbm, ybuf.at[slot, 1], sems.at[slot, 1], rows)
    route = route_ref[...]
    y = route[:, ROUTE_GATE0:ROUTE_GATE0 + 1] * ybuf[slot, 0] + route[:, ROUTE_GATE1:ROUTE_GATE1 + 1] * ybuf[slot, 1]
    out_ref[...] = _rms(x2_ref[...] + y, gfin_ref[...])


def _moe_and_final(x2, u3, route, w_e1, w_e3, w_e2, g_final, tm):
    n, d_model = x2.shape
    n_exp = w_e1.shape[0]
    a_total = n * EXPERT_TOPK
    n_blocks = -(-(a_total + n_exp * (MOE_BLOCK - 1)) // MOE_BLOCK)
    p_rows = n_blocks * MOE_BLOCK

    fe = route[:, ROUTE_EXPERT0:ROUTE_EXPERT1 + 1].astype(I32).reshape(-1)
    onehot = (fe[:, None] == jnp.arange(n_exp, dtype=I32)[None, :]).astype(I32)
    csum = jnp.cumsum(onehot, axis=0)
    rank = jnp.sum((csum - onehot) * onehot, axis=1)
    counts = csum[-1]
    padded = (counts + MOE_BLOCK - 1) // MOE_BLOCK * MOE_BLOCK
    pad_end = jnp.cumsum(padded)
    dest = (pad_end - padded)[fe] + rank
    ft = jnp.arange(a_total, dtype=I32) // EXPERT_TOPK
    slot_tok = jnp.zeros((p_rows,), I32).at[dest].set(ft)
    block_exp = jnp.minimum(jnp.searchsorted(pad_end, jnp.arange(n_blocks, dtype=I32) * MOE_BLOCK, side='right'),
                            n_exp - 1).astype(I32)

    tok3 = slot_tok.reshape(n_blocks, 1, MOE_BLOCK)
    smem_blk = lambda w, f: pl.BlockSpec((1, 1, w), f, memory_space=pltpu.SMEM)
    w_spec = lambda shp: pl.BlockSpec((1,) + shp[1:], lambda i, be: (be[i], 0, 0))
    yb = pl.pallas_call(
        _expert_kernel,
        grid_spec=pltpu.PrefetchScalarGridSpec(
            num_scalar_prefetch=1,
            grid=(n_blocks,),
            in_specs=[smem_blk(MOE_BLOCK, lambda i, be: (i, 0, 0)),
                      smem_blk(MOE_BLOCK, lambda i, be: (jnp.minimum(i + 1, n_blocks - 1), 0, 0)),
                      pl.BlockSpec(memory_space=pl.ANY),
                      w_spec(w_e1.shape), w_spec(w_e3.shape), w_spec(w_e2.shape)],
            out_specs=pl.BlockSpec((MOE_BLOCK, d_model), lambda i, be: (i, 0)),
            scratch_shapes=[pltpu.VMEM((2, MOE_BLOCK, d_model), F32), pltpu.SemaphoreType.DMA((2,))],
        ),
        out_shape=jax.ShapeDtypeStruct((p_rows, d_model), F32),
        compiler_params=_cparams(("arbitrary",)),
        name="moe_experts",
    )(block_exp, tok3, tok3, u3, w_e1.astype(BF16), w_e3.astype(BF16), w_e2.astype(BF16))

    tm = min(tm, n)
    assert n % tm == 0
    nt = n // tm
    d2 = dest.reshape(n, EXPERT_TOPK)
    s0 = d2[:, 0].reshape(nt, 1, tm)
    s1 = d2[:, 1].reshape(nt, 1, tm)
    cur = lambda i: (i, 0, 0)
    nxt = lambda i: (jnp.minimum(i + 1, nt - 1), 0, 0)
    smem_blk2 = lambda f: pl.BlockSpec((1, 1, tm), f, memory_space=pltpu.SMEM)
    row_spec = lambda w: pl.BlockSpec((tm, w), lambda i: (i, 0))
    return pl.pallas_call(
        _combine_kernel,
        grid=(nt,),
        in_specs=[smem_blk2(cur), smem_blk2(cur), smem_blk2(nxt), smem_blk2(nxt),
                  pl.BlockSpec(memory_space=pl.ANY), row_spec(d_model), row_spec(LANES), _const_spec((1, d_model))],
        out_specs=row_spec(d_model),
        out_shape=jax.ShapeDtypeStruct((n, d_model), F32),
        scratch_shapes=[pltpu.VMEM((2, EXPERT_TOPK, tm, d_model), F32), pltpu.SemaphoreType.DMA((2, EXPERT_TOPK))],
        compiler_params=_cparams(("arbitrary",)),
        name="moe_combine",
    )(s0, s1, s0, s1, yb, x2, route, g_final.reshape(1, d_model))


def _layer(x, pos, conv_buf, h0, attend, mk, mv, lw, g_final, tiles):
    batch, t_len, d_model = x.shape
    n = batch * t_len
    d_lru = lw['w_conv'].shape[1]
    d_att = N_HEADS * HEAD_DIM
    d_idx = IDX_HEADS * IDX_DIM
    x2d = x.reshape(n, d_model)
    xl, q, k, v, qi, kiwi, gl, ga = _in_proj(x2d, lw['g_mix'], lw['w_in_packed'], pos, t_len, d_lru, d_att, d_idx,
                                             tiles['in_proj'])
    att = attend(qi, q, kiwi, k, v)
    hs, h_last, new_buf = _rglru(xl.reshape(batch, t_len, d_lru), conv_buf, h0, lw['w_conv'], lw['b_conv'],
                                 lw['w_ra'], lw['b_ra'], lw['w_ri'], lw['b_ri'], lw['lru_lambda'],
                                 tiles['lru_t'], tiles['lru_b'])
    x1, qc = _mix(x2d, hs.reshape(n, d_lru), att, gl, ga, lw['w_lru_out'], lw['w_att_out'], lw['w_mix_out'],
                  lw['g_cross'], lw['w_cq'], tiles['mix'])
    x2, u3, route = _cross(x1, qc, mk, mv, lw['w_co'], lw['g_ffn'], lw['w_router'], lw['b_router'], batch, t_len,
                           tiles['cross'])
    y = _moe_and_final(x2, u3, route, lw['w_e1'], lw['w_e3'], lw['w_e2'], g_final, tiles['combine'])
    k5 = k.reshape(batch, t_len, N_HEADS, HEAD_DIM)
    v5 = v.reshape(batch, t_len, N_HEADS, HEAD_DIM)
    ki = kiwi[:, :IDX_DIM].reshape(batch, t_len, IDX_DIM)
    return y.reshape(batch, t_len, d_model), new_buf, h_last, k5, v5, ki


def kernel(x_prompt, mem_prompt, x_sample, cache_k, cache_v, cache_kidx, cache_mem_k, cache_mem_v, state_conv, state_lru, page_table, g_mix, w_in, w_conv, b_conv, w_ra, b_ra, w_ri, b_ri, lru_lambda, w_lru_out, w_att_out, w_mix_out, g_cross, g_mem, w_cq, w_mk, w_mv, w_co, g_ffn, w_rg, b_rg, w_re, b_re, w_e1, w_e3, w_e2, g_final):
    depth = w_in.shape[0]
    assert depth == 1, "the final norm is fused into the last layer's MoE combine; one layer supported"
    l = 0
    b_p, t_p, _ = x_prompt.shape
    b_s, t_s, _ = x_sample.shape
    past = page_table.shape[1] * cache_k.shape[2]
    d_lru = w_conv.shape[2]
    d_att = N_HEADS * HEAD_DIM
    d_idx = IDX_HEADS * IDX_DIM
    w_router, b_router = _pack_router(w_rg[l], b_rg[l], w_re[l], b_re[l])
    lw = {
        'g_mix': g_mix[l], 'w_in_packed': _pack_w_in(w_in[l], d_lru, d_att, d_idx),
        'w_conv': w_conv[l], 'b_conv': b_conv[l], 'w_ra': w_ra[l], 'b_ra': b_ra[l], 'w_ri': w_ri[l], 'b_ri': b_ri[l],
        'lru_lambda': lru_lambda[l], 'w_lru_out': w_lru_out[l], 'w_att_out': w_att_out[l], 'w_mix_out': w_mix_out[l],
        'g_cross': g_cross[l], 'w_cq': w_cq[l], 'w_co': w_co[l], 'g_ffn': g_ffn[l],
        'w_router': w_router, 'b_router': b_router, 'w_e1': w_e1[l], 'w_e3': w_e3[l], 'w_e2': w_e2[l],
    }

    mk_p, mv_p = _mem_kv(mem_prompt, g_mem[l], w_mk[l], w_mv[l])
    attend_p = functools.partial(_prompt_attention, batch=b_p, t_len=t_p)
    tiles_p = {'in_proj': 512, 'lru_t': 128, 'lru_b': 8, 'mix': 512, 'cross': 256, 'combine': 256}
    y_p, buf_p, h_p, k_p, v_p, ki_p = _layer(
        x_prompt, jnp.arange(t_p), jnp.zeros((b_p, CONV_W - 1, d_lru), F32), jnp.zeros((b_p, d_lru), F32),
        attend_p, mk_p, mv_p, lw, g_final, tiles_p)

    def attend_s(qi, q, kiwi, k, v):
        return _sample_attention(qi, q, kiwi, k, v, cache_k, cache_v, cache_kidx, page_table, l, t_s, pg=8)
    tiles_s = {'in_proj': 512, 'lru_t': 8, 'lru_b': 8, 'mix': 512, 'cross': 8, 'combine': 256}
    y_s, buf_s, h_s, k_s, v_s, ki_s = _layer(
        x_sample, past + jnp.arange(t_s), state_conv[l], state_lru[l], attend_s, cache_mem_k[l], cache_mem_v[l],
        lw, g_final, tiles_s)

    st = lambda a: a[None]
    return (y_p, y_s, st(k_p), st(v_p), st(ki_p), st(buf_p), st(h_p), st(mk_p), st(mv_p),
            st(k_s), st(v_s), st(ki_s), st(buf_s), st(h_s))
```

```python
import functools
import math

import jax
import jax.numpy as jnp
import numpy as np
from jax import lax
from jax.experimental import pallas as pl
from jax.experimental.pallas import tpu as pltpu

F32 = jnp.float32
BF16 = jnp.bfloat16
I32 = jnp.int32

N_HEADS = 8
HEAD_DIM = 64
IDX_HEADS = 8
IDX_DIM = 64
TOPK_MAX = 256
Q_BLOCK = 128
ROPE_THETA = 10000.0
CONV_W = 4
LRU_C = 8.0
LRU_BLOCKS = 8
C_HEADS = 4
N_GROUPS = 4
EXPERTS_PER_GROUP = 8
EXPERT_TOPK = 2
MOE_BLOCK = 128
NORM_EPS = 1e-6

LANES = 128
VMEM_LIMIT = 56 * 1024 * 1024


def _cparams(sem):
    return pltpu.CompilerParams(dimension_semantics=sem, vmem_limit_bytes=VMEM_LIMIT)


def _const_spec(shape):
    nd = len(shape)
    return pl.BlockSpec(shape, lambda *_: (0,) * nd)


def _rms(x, g):
    return x * lax.rsqrt(jnp.mean(x * x, axis=-1, keepdims=True) + NORM_EPS) * g


def _rope_rot(x, cos, sin_signed):
    w = x.shape[1]
    reps = w // LANES
    fwd = pltpu.roll(x, 32, axis=1)
    bwd = pltpu.roll(x, w - 32, axis=1)
    lane = lax.broadcasted_iota(I32, x.shape, 1)
    rot = jnp.where((lane % 64) < 32, bwd, fwd)
    if reps > 1:
        cos = jnp.concatenate([cos] * reps, axis=1)
        sin_signed = jnp.concatenate([sin_signed] * reps, axis=1)
    return x * cos + rot * sin_signed


def _in_proj_kernel(x_ref, g_ref, w_ref, cos_ref, sin_ref,
                    xl_ref, q_ref, k_ref, v_ref, qi_ref, kiwi_ref, gl_ref, ga_ref, *, d_lru, d_att, d_idx, d_model):
    u = _rms(x_ref[...], g_ref[...]).astype(BF16)
    cos = cos_ref[...]
    sin = sin_ref[...]

    def proj(c0, width):
        return jnp.dot(u, w_ref[:, c0:c0 + width], preferred_element_type=F32)

    c = 0
    xl_ref[...] = proj(c, d_lru); c += d_lru
    q_ref[...] = (_rope_rot(proj(c, d_att), cos, sin) * (HEAD_DIM ** -0.5)).astype(BF16); c += d_att
    k_ref[...] = _rope_rot(proj(c, d_att), cos, sin); c += d_att
    v_ref[...] = proj(c, d_att); c += d_att
    qi_ref[...] = (_rope_rot(proj(c, d_idx), cos, sin) * (IDX_DIM ** -0.5)).astype(BF16); c += d_idx
    kw = proj(c, LANES); c += LANES
    lane = lax.broadcasted_iota(I32, kw.shape, 1)
    kiwi_ref[...] = jnp.where(lane < IDX_DIM, _rope_rot(kw, cos, sin), kw * (IDX_HEADS ** -0.5))
    gl_ref[...] = jax.nn.sigmoid(proj(c, d_model)); c += d_model
    ga_ref[...] = jax.nn.sigmoid(proj(c, d_model))


def _rope_tables(pos, rows):
    inv = ROPE_THETA ** (-jnp.arange(0, HEAD_DIM, 2, dtype=F32) / HEAD_DIM)
    ang = pos.astype(F32)[:, None] * inv[None, :]
    cos = jnp.cos(ang)
    sin = jnp.sin(ang)
    cos_t = jnp.concatenate([cos, cos, cos, cos], axis=1)
    sin_t = jnp.concatenate([-sin, sin, -sin, sin], axis=1)
    reps = rows // pos.shape[0]
    if reps > 1:
        cos_t = jnp.tile(cos_t, (reps, 1))
        sin_t = jnp.tile(sin_t, (reps, 1))
    return cos_t, sin_t


def _in_proj(x2d, g, w_packed, pos, seq_len, d_lru, d_att, d_idx, tm):
    n, d_model = x2d.shape
    tm = min(tm, n)
    assert n % tm == 0
    rows = max(seq_len, tm)
    assert rows % tm == 0 and (tm % seq_len == 0 or seq_len % tm == 0)
    cos_t, sin_t = _rope_tables(pos, rows)
    nt = rows // tm
    row_spec = lambda w: pl.BlockSpec((tm, w), lambda i: (i, 0))
    tab_spec = pl.BlockSpec((tm, LANES), lambda i: (i % nt, 0))
    kern = functools.partial(_in_proj_kernel, d_lru=d_lru, d_att=d_att, d_idx=d_idx, d_model=d_model)
    out_shape = (
        jax.ShapeDtypeStruct((n, d_lru), F32),
        jax.ShapeDtypeStruct((n, d_att), BF16),
        jax.ShapeDtypeStruct((n, d_att), F32),
        jax.ShapeDtypeStruct((n, d_att), F32),
        jax.ShapeDtypeStruct((n, d_idx), BF16),
        jax.ShapeDtypeStruct((n, LANES), F32),
        jax.ShapeDtypeStruct((n, d_model), F32),
        jax.ShapeDtypeStruct((n, d_model), F32),
    )
    return pl.pallas_call(
        kern,
        grid=(n // tm,),
        in_specs=[row_spec(d_model), _const_spec((1, d_model)), _const_spec(w_packed.shape), tab_spec, tab_spec],
        out_specs=tuple(row_spec(s.shape[1]) for s in out_shape),
        out_shape=out_shape,
        compiler_params=_cparams(("parallel",)),
        name="in_proj",
    )(x2d, g.reshape(1, d_model), w_packed, cos_t, sin_t)


def _pack_w_in(w_in, d_lru, d_att, d_idx):
    d_model = w_in.shape[0]
    c = d_lru + 3 * d_att + d_idx
    kiwi = w_in[:, c:c + IDX_DIM + IDX_HEADS]
    pad = jnp.zeros((d_model, LANES - IDX_DIM - IDX_HEADS), w_in.dtype)
    return jnp.concatenate([w_in[:, :c], kiwi, pad, w_in[:, c + IDX_DIM + IDX_HEADS:]], axis=1).astype(BF16)


CONV_PAD = 8


def _softplus(x):
    return jnp.maximum(x, 0.0) + jnp.log1p(jnp.exp(-jnp.abs(x)))


def _rglru_kernel(xl_ref, buf0_ref, h0_ref, wconv_ref, bconv_ref, wra_ref, bra_ref, wri_ref, bri_ref, lam_ref,
                  hs_ref, hlast_ref, newbuf_ref, xpad_scr, a_scr, u_scr, h_scr, *, tc, bb):
    j = pl.program_id(1)
    d = xl_ref.shape[-1]
    hist = CONV_W - 1

    @pl.when(j == 0)
    def _():
        xpad_scr[:, 0:CONV_PAD - hist, :] = jnp.zeros((bb, CONV_PAD - hist, d), F32)
        xpad_scr[:, CONV_PAD - hist:CONV_PAD, :] = buf0_ref[...]
        h_scr[...] = h0_ref[...]

    xpad_scr[:, CONV_PAD:CONV_PAD + tc, :] = xl_ref[...]
    xc = jnp.zeros((bb, tc, d), F32) + bconv_ref[...]
    for i in range(CONV_W):
        off = CONV_PAD - hist + i
        xc = xc + xpad_scr[:, off:off + tc, :] * wconv_ref[i:i + 1, :]
    newbuf_ref[...] = xpad_scr[:, CONV_PAD + tc - hist:CONV_PAD + tc, :]
    xpad_scr[:, 0:CONV_PAD, :] = xpad_scr[:, tc:tc + CONV_PAD, :]

    xc2 = xc.reshape(bb * tc, d)
    xb = xc2.astype(BF16)
    r = jax.nn.sigmoid(jnp.dot(xb, wra_ref[...], preferred_element_type=F32) + bra_ref[...])
    g = jax.nn.sigmoid(jnp.dot(xb, wri_ref[...], preferred_element_type=F32) + bri_ref[...])
    log_a = (-LRU_C) * r * _softplus(-lam_ref[...])
    a = jnp.exp(log_a)
    mult = jnp.sqrt(-jnp.tanh(log_a) * (a * a + 1.0))
    a_scr[...] = a.reshape(bb, tc, d)
    u_scr[...] = (mult * (g * xc2)).reshape(bb, tc, d)

    def step(t, hs):
        new = []
        for b in range(bb):
            h = a_scr[b, pl.ds(t, 1), :] * hs[b] + u_scr[b, pl.ds(t, 1), :]
            u_scr[b, pl.ds(t, 1), :] = h
            new.append(h)
        return tuple(new)

    h_fin = lax.fori_loop(0, tc, step, tuple(h_scr[b] for b in range(bb)))
    for b in range(bb):
        h_scr[b] = h_fin[b]
        hlast_ref[b] = h_fin[b]
    hs_ref[...] = u_scr[...].astype(hs_ref.dtype)


def _block_diag(w):
    nb, c, _ = w.shape
    eye = jnp.eye(nb, dtype=w.dtype)
    return (eye[:, None, :, None] * w[:, :, None, :]).reshape(nb * c, nb * c)


def _rglru(xl, buf0, h0, w_conv, b_conv, w_ra, b_ra, w_ri, b_ri, lam, tc, bb):
    b, t, d = xl.shape
    tc = min(tc, t)
    bb = min(bb, b)
    assert t % tc == 0 and b % bb == 0 and tc % 8 == 0
    row = lambda a: a.reshape(1, d)
    kern = functools.partial(_rglru_kernel, tc=tc, bb=bb)
    hist = CONV_W - 1
    out_shape = (
        jax.ShapeDtypeStruct((b, t, d), BF16 if tc % 16 == 0 else F32),
        jax.ShapeDtypeStruct((b, 1, d), F32),
        jax.ShapeDtypeStruct((b, hist, d), F32),
    )
    hs, h_last, new_buf = pl.pallas_call(
        kern,
        grid=(b // bb, t // tc),
        in_specs=[
            pl.BlockSpec((bb, tc, d), lambda i, j: (i, j, 0)),
            pl.BlockSpec((bb, hist, d), lambda i, j: (i, 0, 0)),
            pl.BlockSpec((bb, 1, d), lambda i, j: (i, 0, 0)),
            _const_spec((CONV_W, d)), _const_spec((1, d)),
            _const_spec((d, d)), _const_spec((1, d)), _const_spec((d, d)), _const_spec((1, d)), _const_spec((1, d)),
        ],
        out_specs=(
            pl.BlockSpec((bb, tc, d), lambda i, j: (i, j, 0)),
            pl.BlockSpec((bb, 1, d), lambda i, j: (i, 0, 0)),
            pl.BlockSpec((bb, hist, d), lambda i, j: (i, 0, 0)),
        ),
        out_shape=out_shape,
        scratch_shapes=[
            pltpu.VMEM((bb, tc + CONV_PAD, d), F32),
            pltpu.VMEM((bb, tc, d), F32),
            pltpu.VMEM((bb, tc, d), F32),
            pltpu.VMEM((bb, 1, d), F32),
        ],
        compiler_params=_cparams(("parallel", "arbitrary")),
        name="rglru",
    )(xl, buf0, h0.reshape(b, 1, d), w_conv, row(b_conv),
      _block_diag(w_ra).astype(BF16), row(b_ra), _block_diag(w_ri).astype(BF16), row(b_ri), row(lam))
    return hs, h_last.reshape(b, d), new_buf


INT_MIN = -2 ** 31
NEG_INF_KEY = -2139095041
KEY_GROUP = 4


def _sortable_key(score):
    score = jnp.where(score == 0.0, 0.0, score)
    bits = pltpu.bitcast(score, I32)
    return bits ^ ((bits >> 31) & 0x7FFFFFFF)


def _pair_block_diag(x):
    lane = lax.broadcasted_iota(I32, x.shape, 1)
    zero = jnp.zeros_like(x)
    return jnp.concatenate([jnp.where(lane < 64, x, zero), jnp.where(lane >= 64, x, zero)], axis=0)


def _dot_nt(a, b):
    return lax.dot_general(a, b, (((1,), (1,)), ((), ())), preferred_element_type=F32)


def _selection_bias(kk, eq, tie_rank, theta, need):
    take = jnp.where(eq, tie_rank, jnp.inf) <= need
    b = jnp.where(take, 0.0, -jnp.inf)
    b = jnp.where(kk > theta, 0.0, b)
    return jnp.where(kk == NEG_INF_KEY, -jnp.inf, b)


def _prompt_attn_kernel(qi_ref, q_ref, kiwi_ref, k_ref, v_ref, att_ref,
                        kdup_scr, kbf_scr, vT_scr, qibd_scr, qbd_scr, key_scr, bias_scr, logit_scr, oT_scr, acc_scr,
                        *, topk, qb):
    j = pl.program_id(1)
    t_len = k_ref.shape[0]
    n_pairs = q_ref.shape[1] // LANES
    nk = j + 1

    def rows_of(c):
        return pl.ds(pl.multiple_of(c * qb, qb), qb)

    @pl.when(j == 0)
    def _():
        def prep(c, carry):
            rows = rows_of(c)
            kw = kiwi_ref[rows, :]
            lane = lax.broadcasted_iota(I32, kw.shape, 1)
            kdup_scr[rows, :] = jnp.where(lane < IDX_DIM, kw, pltpu.roll(kw, IDX_DIM, axis=1)).astype(BF16)
            kbf_scr[rows, :] = k_ref[rows, :].astype(BF16)
            vT_scr[:, rows] = v_ref[rows, :].T.astype(BF16)
            return carry
        lax.fori_loop(0, t_len // qb, prep, 0)

    for p in range(n_pairs):
        qibd_scr[p] = _pair_block_diag(qi_ref[:, p * LANES:(p + 1) * LANES])
        qbd_scr[p] = _pair_block_diag(q_ref[:, p * LANES:(p + 1) * LANES])
    w_t = kiwi_ref[rows_of(j), :].T

    def key_loop(body, init):
        wide = KEY_GROUP * qb
        n_groups = (nk + KEY_GROUP - 1) // KEY_GROUP
        return lax.fori_loop(0, n_groups, lambda i, c: body(pl.multiple_of(i * wide, wide), wide, c), init)

    def idx_body(r0, nr, carry):
        rows = pl.ds(r0, nr)
        kd = kdup_scr[rows, :]
        sc = jnp.zeros((nr, qb), F32)
        for p in range(n_pairs):
            s2 = jnp.maximum(_dot_nt(kd, qibd_scr[p]), 0.0)
            h = IDX_DIM + 2 * p
            sc = sc + w_t[h:h + 1, :] * s2[:, :qb] + w_t[h + 1:h + 2, :] * s2[:, qb:]
        kpos = r0 + lax.broadcasted_iota(I32, (nr, qb), 0)
        qpos = j * qb + lax.broadcasted_iota(I32, (nr, qb), 1)
        key_scr[rows, :] = _sortable_key(jnp.where(kpos <= qpos, sc, -jnp.inf))
        return carry
    key_loop(idx_body, 0)

    def count(pred):
        def body(r0, nr, acc):
            hit = jnp.where(pred(key_scr[pl.ds(r0, nr), :]), 1, 0).astype(I32)
            return acc + jnp.sum(hit.reshape(nr // 8, 8, qb), axis=0)
        return jnp.sum(key_loop(body, jnp.zeros((8, qb), I32)), axis=0, keepdims=True)

    def bit_step(i, prefix):
        cand_u = prefix | jnp.left_shift(jnp.int32(1), 31 - i)
        cand_s = cand_u ^ INT_MIN
        return jnp.where(count(lambda kk: kk >= cand_s) >= topk, cand_u, prefix)
    theta = lax.fori_loop(0, 32, bit_step, jnp.zeros((1, qb), I32)) ^ INT_MIN

    need = (topk - count(lambda kk: kk > theta)).astype(F32)
    tri = jnp.where(lax.broadcasted_iota(I32, (qb, qb), 0) >= lax.broadcasted_iota(I32, (qb, qb), 1),
                    1.0, 0.0).astype(BF16)

    def mask_body(r0, nr, run):
        for s in range(nr // qb):
            rows = pl.ds(r0 + s * qb, qb)
            kk = key_scr[rows, :]
            eq = kk == theta
            pre = jnp.dot(tri, jnp.where(eq, 1.0, 0.0).astype(BF16), preferred_element_type=F32)
            bias_scr[rows, :] = _selection_bias(kk, eq, run + pre, theta, need)
            run = run + pre[qb - 1:qb, :]
        return run
    key_loop(mask_body, jnp.zeros((1, qb), F32))

    def pair_lanes(p):
        return slice(p * LANES, (p + 1) * LANES)

    def pass1(r0, nr, m8s):
        rows = pl.ds(r0, nr)
        bias = bias_scr[rows, :]
        bias2 = jnp.concatenate([bias, bias], axis=1)
        new = []
        for p in range(n_pairs):
            lg = _dot_nt(kbf_scr[rows, pair_lanes(p)], qbd_scr[p]) + bias2
            logit_scr[p, rows, :] = lg
            new.append(jnp.maximum(m8s[p], jnp.max(lg.reshape(nr // 8, 8, 2 * qb), axis=0)))
        return tuple(new)
    m8s = key_loop(pass1, tuple(jnp.full((8, 2 * qb), -jnp.inf, F32) for _ in range(n_pairs)))
    ms = [jnp.max(m8, axis=0, keepdims=True) for m8 in m8s]

    acc_scr[...] = jnp.zeros(acc_scr.shape, F32)

    def pass2(r0, nr, l8s):
        rows = pl.ds(r0, nr)
        new = []
        for p in range(n_pairs):
            pe = jnp.exp(logit_scr[p, rows, :] - ms[p])
            new.append(l8s[p] + jnp.sum(pe.reshape(nr // 8, 8, 2 * qb), axis=0))
            acc_scr[p] += jnp.dot(vT_scr[pair_lanes(p), rows], pe.astype(BF16), preferred_element_type=F32)
        return tuple(new)
    l8s = key_loop(pass2, tuple(jnp.zeros((8, 2 * qb), F32) for _ in range(n_pairs)))

    for p in range(n_pairs):
        o = acc_scr[p] / jnp.sum(l8s[p], axis=0, keepdims=True)
        oT_scr[p * LANES:p * LANES + HEAD_DIM, :] = o[:HEAD_DIM, :qb]
        oT_scr[p * LANES + HEAD_DIM:(p + 1) * LANES, :] = o[HEAD_DIM:, qb:]
    att_ref[...] = oT_scr[...].T.astype(att_ref.dtype)


def _prompt_attention(qi, q, kiwi, k, v, batch, t_len):
    n, d_att = q.shape
    qb = min(Q_BLOCK, t_len)
    assert qb == LANES and t_len % (KEY_GROUP * qb) == 0
    nqb = t_len // qb
    topk = min(TOPK_MAX, t_len // 4)
    n_pairs = d_att // LANES
    kern = functools.partial(_prompt_attn_kernel, topk=topk, qb=qb)
    blk_spec = lambda w: pl.BlockSpec((qb, w), lambda b, j: (b * nqb + j, 0))
    seq_spec = lambda w: pl.BlockSpec((t_len, w), lambda b, j: (b, 0))
    return pl.pallas_call(
        kern,
        grid=(batch, nqb),
        in_specs=[blk_spec(qi.shape[1]), blk_spec(d_att), seq_spec(LANES), seq_spec(d_att), seq_spec(d_att)],
        out_specs=blk_spec(d_att),
        out_shape=jax.ShapeDtypeStruct((n, d_att), BF16),
        scratch_shapes=[
            pltpu.VMEM((t_len, LANES), BF16),
            pltpu.VMEM((t_len, d_att), BF16),
            pltpu.VMEM((d_att, t_len), BF16),
            pltpu.VMEM((n_pairs, 2 * qb, LANES), BF16),
            pltpu.VMEM((n_pairs, 2 * qb, LANES), BF16),
            pltpu.VMEM((t_len, qb), I32),
            pltpu.VMEM((t_len, qb), F32),
            pltpu.VMEM((n_pairs, t_len, 2 * qb), F32),
            pltpu.VMEM((d_att, qb), F32),
            pltpu.VMEM((n_pairs, LANES, 2 * qb), F32),
        ],
        compiler_params=_cparams(("parallel", "arbitrary")),
        name="prompt_attn",
    )(qi, q, kiwi, k, v)


def _sample_select_kernel(pt_ref, qi_ref, wi_ref, kinew_ref, *rest, topk, pg, past, page, t_new):
    del pt_ref
    page_refs = rest[:pg]
    bias_ref = rest[pg]
    key_scr = rest[pg + 1]
    g = pl.program_id(1)
    n_chunks = past // page + 1
    qi = qi_ref[...]
    wi = wi_ref[...]

    def head_sum(s):
        s = jnp.maximum(s, 0.0) * wi
        return jnp.sum(s.reshape(IDX_HEADS, t_new, s.shape[1]), axis=0)

    k_t = jnp.concatenate([r[0, 0] for r in page_refs], axis=1).astype(BF16)
    off = pl.multiple_of(g * (pg * page), page)
    key_scr[:, pl.ds(off, pg * page)] = _sortable_key(head_sum(jnp.dot(qi, k_t, preferred_element_type=F32)))

    @pl.when(g == pl.num_programs(1) - 1)
    def _():
        knew = jnp.concatenate([kinew_ref[:, :IDX_DIM], jnp.zeros((page - t_new, IDX_DIM), F32)], axis=0)
        sc = head_sum(_dot_nt(qi, knew.astype(BF16)))
        tok = lax.broadcasted_iota(I32, sc.shape, 0)
        kk_i = lax.broadcasted_iota(I32, sc.shape, 1)
        key_scr[:, past:past + page] = _sortable_key(jnp.where(kk_i <= tok, sc, -jnp.inf))

        def chunk(c):
            return key_scr[:, c * page:(c + 1) * page]

        def count(pred):
            acc = jnp.zeros((t_new, page), I32)
            for c in range(n_chunks):
                acc = acc + jnp.where(pred(chunk(c)), 1, 0).astype(I32)
            return jnp.sum(acc, axis=1, keepdims=True)

        def bit_step(i, prefix):
            cand_u = prefix | jnp.left_shift(jnp.int32(1), 31 - i)
            cand_s = cand_u ^ INT_MIN
            return jnp.where(count(lambda kk: kk >= cand_s) >= topk, cand_u, prefix)
        theta = lax.fori_loop(0, 32, bit_step, jnp.zeros((t_new, 1), I32)) ^ INT_MIN
        need = (topk - count(lambda kk: kk > theta)).astype(F32)

        row_i = lax.broadcasted_iota(I32, (page, page), 0)
        col_i = lax.broadcasted_iota(I32, (page, page), 1)
        triu = jnp.where(row_i <= col_i, 1.0, 0.0).astype(BF16)
        run = jnp.zeros((t_new, 1), F32)
        for c in range(n_chunks):
            kk = chunk(c)
            eq = kk == theta
            pre = jnp.dot(jnp.where(eq, 1.0, 0.0).astype(BF16), triu, preferred_element_type=F32)
            bias_ref[0, :, c * page:(c + 1) * page] = _selection_bias(kk, eq, run + pre, theta, need)
            run = run + pre[:, page - 1:page]


def _sample_attn_kernel(pt_ref, q_ref, bias_ref, knew_ref, vnew_ref, *rest, pg, past, page, t_new):
    del pt_ref
    k_refs = rest[:pg]
    v_refs = rest[pg:2 * pg]
    out_ref = rest[2 * pg]
    m_scr, l_scr, acc_scr = rest[2 * pg + 1:]
    g = pl.program_id(1)
    qbd = q_ref[...]

    @pl.when(g == 0)
    def _():
        m_scr[...] = jnp.full(m_scr.shape, -jnp.inf, F32)
        l_scr[...] = jnp.zeros(l_scr.shape, F32)
        acc_scr[...] = jnp.zeros(acc_scr.shape, F32)

    def attend(lg, bias, pv_fn):
        lg = lg + jnp.concatenate([bias] * N_HEADS, axis=0)
        m_old = m_scr[...]
        m_new = jnp.maximum(m_old, jnp.max(lg, axis=1, keepdims=True))
        m_safe = jnp.where(m_new == -jnp.inf, 0.0, m_new)
        alpha = jnp.exp(m_old - m_safe)
        pe = jnp.exp(lg - m_safe)
        l_scr[...] = alpha * l_scr[...] + jnp.sum(pe, axis=1, keepdims=True)
        acc_scr[...] = alpha * acc_scr[...] + pv_fn(pe.astype(BF16))
        m_scr[...] = m_new

    k_t = jnp.concatenate([r[0, 0] for r in k_refs], axis=1).astype(BF16)
    v_t = jnp.concatenate([r[0, 0] for r in v_refs], axis=1).astype(BF16)
    off = pl.multiple_of(g * (pg * page), page)
    attend(jnp.dot(qbd, k_t, preferred_element_type=F32), bias_ref[0, :, pl.ds(off, pg * page)],
           lambda pb: _dot_nt(pb, v_t))

    @pl.when(g == pl.num_programs(1) - 1)
    def _():
        pad = jnp.zeros((page - t_new, knew_ref.shape[1]), F32)
        k_new = jnp.concatenate([knew_ref[...], pad], axis=0).astype(BF16)
        v_new = jnp.concatenate([vnew_ref[...], pad], axis=0).astype(BF16)
        attend(_dot_nt(qbd, k_new), bias_ref[0, :, past:past + page],
               lambda pb: jnp.dot(pb, v_new, preferred_element_type=F32))
        out_ref[0] = acc_scr[...] / l_scr[...]


def _to_head_major(x, batch, t_new, heads, dim):
    return x.reshape(batch, t_new, heads, dim).transpose(0, 2, 1, 3).reshape(batch * heads * t_new, dim)


def _sample_attention(qi, q, kiwi, k_new, v_new, cache_k, cache_v, cache_kidx, page_table, layer, t_new, pg_sel, pg_att):
    batch, n_pages = page_table.shape
    depth, n_pool, page = cache_k.shape[:3]
    past = n_pages * page
    pg_sel = min(pg_sel, n_pages)
    pg_att = min(pg_att, n_pages)
    assert n_pages % pg_sel == 0 and n_pages % pg_att == 0 and page == LANES and t_new == 8
    l_pad = past + page
    topk = min(TOPK_MAX, (past + t_new) // 4)
    rows = IDX_HEADS * t_new
    d_att = N_HEADS * HEAD_DIM

    kidx_t = cache_kidx.transpose(0, 1, 3, 2)
    k_t = cache_k.transpose(0, 1, 3, 4, 2).reshape(depth, n_pool, d_att, page)
    v_t = cache_v.transpose(0, 1, 3, 4, 2).reshape(depth, n_pool, d_att, page)

    qi_hm = _to_head_major(qi, batch, t_new, IDX_HEADS, IDX_DIM)
    wi_col = kiwi[:, IDX_DIM:IDX_DIM + IDX_HEADS].reshape(batch, t_new, IDX_HEADS).transpose(0, 2, 1).reshape(-1, 1)
    seq_spec = lambda r, w: pl.BlockSpec((r, w), lambda b, g, pt: (b, 0))
    bias_spec = pl.BlockSpec((1, t_new, l_pad), lambda b, g, pt: (b, 0, 0))

    def page_spec(arr, i, pg):
        return pl.BlockSpec((1, 1) + arr.shape[2:], lambda b, g, pt: (layer, pt[b, g * pg + i], 0, 0))

    bias = pl.pallas_call(
        functools.partial(_sample_select_kernel, topk=topk, pg=pg_sel, past=past, page=page, t_new=t_new),
        grid_spec=pltpu.PrefetchScalarGridSpec(
            num_scalar_prefetch=1,
            grid=(batch, n_pages // pg_sel),
            in_specs=[seq_spec(rows, IDX_DIM), seq_spec(rows, 1), seq_spec(t_new, LANES)]
                     + [page_spec(kidx_t, i, pg_sel) for i in range(pg_sel)],
            out_specs=bias_spec,
            scratch_shapes=[pltpu.VMEM((t_new, l_pad), I32)],
        ),
        out_shape=jax.ShapeDtypeStruct((batch, t_new, l_pad), F32),
        compiler_params=_cparams(("parallel", "arbitrary")),
        name="sample_select",
    )(page_table, qi_hm, wi_col, kiwi, *([kidx_t] * pg_sel))

    q4 = q.reshape(batch, t_new, N_HEADS, HEAD_DIM)
    eye = jnp.eye(N_HEADS, dtype=q.dtype)
    qbd = (q4.transpose(0, 2, 1, 3)[:, :, :, None, :] * eye[None, :, None, :, None]).reshape(batch * rows, d_att)
    out = pl.pallas_call(
        functools.partial(_sample_attn_kernel, pg=pg_att, past=past, page=page, t_new=t_new),
        grid_spec=pltpu.PrefetchScalarGridSpec(
            num_scalar_prefetch=1,
            grid=(batch, n_pages // pg_att),
            in_specs=[seq_spec(rows, d_att), bias_spec, seq_spec(t_new, d_att), seq_spec(t_new, d_att)]
                     + [page_spec(k_t, i, pg_att) for i in range(pg_att)]
                     + [page_spec(v_t, i, pg_att) for i in range(pg_att)],
            out_specs=pl.BlockSpec((1, rows, d_att), lambda b, g, pt: (b, 0, 0)),
            scratch_shapes=[pltpu.VMEM((rows, 1), F32), pltpu.VMEM((rows, 1), F32), pltpu.VMEM((rows, d_att), F32)],
        ),
        out_shape=jax.ShapeDtypeStruct((batch, rows, d_att), F32),
        compiler_params=_cparams(("parallel", "arbitrary")),
        name="sample_attn",
    )(page_table, qbd, bias, k_new, v_new, *([k_t] * pg_att), *([v_t] * pg_att))
    o5 = out.reshape(batch, N_HEADS, t_new, N_HEADS, HEAD_DIM)
    att = jnp.stack([o5[:, h, :, h, :] for h in range(N_HEADS)], axis=2)
    return att.reshape(batch * t_new, d_att).astype(BF16)


def _mix_kernel(x_ref, hs_ref, att_ref, gl_ref, ga_ref, wl_ref, wa_ref, wm_ref, gc_ref, wcq_ref, x1_ref, qc_ref):
    lru = jnp.dot(hs_ref[...].astype(BF16), wl_ref[...], preferred_element_type=F32)
    att = jnp.dot(att_ref[...], wa_ref[...], preferred_element_type=F32)
    mixed = gl_ref[...] * lru + ga_ref[...] * att
    x1 = x_ref[...] + jnp.dot(mixed.astype(BF16), wm_ref[...], preferred_element_type=F32)
    x1_ref[...] = x1
    qc_ref[...] = jnp.dot(_rms(x1, gc_ref[...]).astype(BF16), wcq_ref[...], preferred_element_type=F32)


def _mix(x2d, hs, att, gl, ga, w_lru_out, w_att_out, w_mix_out, g_cross, w_cq, tm):
    n, d_model = x2d.shape
    tm = min(tm, n)
    assert n % tm == 0
    d_cross = w_cq.shape[1]
    row_spec = lambda w: pl.BlockSpec((tm, w), lambda i: (i, 0))
    return pl.pallas_call(
        _mix_kernel,
        grid=(n // tm,),
        in_specs=[row_spec(d_model), row_spec(hs.shape[1]), row_spec(att.shape[1]), row_spec(d_model), row_spec(d_model),
                  _const_spec(w_lru_out.shape), _const_spec(w_att_out.shape), _const_spec(w_mix_out.shape),
                  _const_spec((1, d_model)), _const_spec(w_cq.shape)],
        out_specs=(row_spec(d_model), row_spec(d_cross)),
        out_shape=(jax.ShapeDtypeStruct((n, d_model), F32), jax.ShapeDtypeStruct((n, d_cross), F32)),
        compiler_params=_cparams(("parallel",)),
        name="mix_out",
    )(x2d, hs, att, gl, ga, w_lru_out.astype(BF16), w_att_out.astype(BF16), w_mix_out.astype(BF16),
      g_cross.reshape(1, d_model), w_cq.astype(BF16))


def _mem_kv_kernel(mem_ref, g_ref, wk_ref, wv_ref, mk_ref, mv_ref):
    m = _rms(mem_ref[0], g_ref[...]).astype(BF16)
    mk = jnp.dot(m, wk_ref[...], preferred_element_type=F32)
    mv = jnp.dot(m, wv_ref[...], preferred_element_type=F32)
    hd = mk_ref.shape[-1]
    for h in range(C_HEADS):
        mk_ref[0, :, h, :] = mk[:, h * hd:(h + 1) * hd]
        mv_ref[0, :, h, :] = mv[:, h * hd:(h + 1) * hd]


def _mem_kv(mem, g_mem, w_mk, w_mv):
    b, s, d_model = mem.shape
    hd = w_mk.shape[1] // C_HEADS
    kv_spec = pl.BlockSpec((1, s, C_HEADS, hd), lambda i: (i, 0, 0, 0))
    kv_shape = jax.ShapeDtypeStruct((b, s, C_HEADS, hd), F32)
    return pl.pallas_call(
        _mem_kv_kernel,
        grid=(b,),
        in_specs=[pl.BlockSpec((1, s, d_model), lambda i: (i, 0, 0)), _const_spec((1, d_model)),
                  _const_spec(w_mk.shape), _const_spec(w_mv.shape)],
        out_specs=(kv_spec, kv_spec),
        out_shape=(kv_shape, kv_shape),
        compiler_params=_cparams(("parallel",)),
        name="mem_kv",
    )(mem, g_mem.reshape(1, d_model), w_mk.astype(BF16), w_mv.astype(BF16))


ROUTE_EXPERT0, ROUTE_EXPERT1, ROUTE_GATE0, ROUTE_GATE1 = 0, 1, 2, 3


def _lane_min_where(cond, lane):
    return jnp.min(jnp.where(cond, lane, LANES), axis=1, keepdims=True)


def _cross_kernel(x1_ref, qc_ref, mk_ref, mv_ref, wco_ref, gf_ref, wr_ref, br_ref, x2_ref, u3_ref, route_ref):
    hd = mk_ref.shape[-1]
    scale = hd ** -0.5
    outs = []
    for h in range(C_HEADS):
        mk = mk_ref[0, :, h, :].astype(BF16)
        mv = mv_ref[0, :, h, :].astype(BF16)
        s = _dot_nt(qc_ref[:, h * hd:(h + 1) * hd].astype(BF16), mk) * scale
        pe = jnp.exp(s - jnp.max(s, axis=1, keepdims=True))
        o = jnp.dot(pe.astype(BF16), mv, preferred_element_type=F32)
        outs.append(o / jnp.sum(pe, axis=1, keepdims=True))
    o = jnp.concatenate(outs, axis=1).astype(BF16)
    x2 = x1_ref[...] + jnp.dot(o, wco_ref[...], preferred_element_type=F32)
    x2_ref[...] = x2
    u3 = _rms(x2, gf_ref[...])
    for s in range(u3_ref.shape[1]):
        u3_ref[:, s, :] = u3[:, s * LANES:(s + 1) * LANES]

    logits = jnp.dot(u3.astype(BF16), wr_ref[...], preferred_element_type=F32) + br_ref[...]
    lane = lax.broadcasted_iota(I32, logits.shape, 1)
    lg = jnp.where(lane < N_GROUPS, logits, -jnp.inf)
    g_max = jnp.max(lg, axis=1, keepdims=True)
    g_sel = _lane_min_where(lg == g_max, lane)
    p_grp = 1.0 / jnp.sum(jnp.exp(lg - g_max), axis=1, keepdims=True)
    e_lo = N_GROUPS + g_sel * EXPERTS_PER_GROUP
    le = jnp.where((lane >= e_lo) & (lane < e_lo + EXPERTS_PER_GROUP), logits, -jnp.inf)
    m1 = jnp.max(le, axis=1, keepdims=True)
    i1 = _lane_min_where(le == m1, lane)
    le2 = jnp.where(lane == i1, -jnp.inf, le)
    m2 = jnp.max(le2, axis=1, keepdims=True)
    i2 = _lane_min_where(le2 == m2, lane)
    e2 = jnp.exp(m2 - m1)
    inv = p_grp / (1.0 + e2)
    route = jnp.where(lane == ROUTE_EXPERT0, (i1 - N_GROUPS).astype(F32), 0.0)
    route = jnp.where(lane == ROUTE_EXPERT1, (i2 - N_GROUPS).astype(F32), route)
    route = jnp.where(lane == ROUTE_GATE0, inv, route)
    route_ref[...] = jnp.where(lane == ROUTE_GATE1, inv * e2, route)


def _pack_router(w_rg, b_rg, w_re, b_re):
    d_model = w_rg.shape[0]
    used = w_rg.shape[1] + w_re.shape[1]
    w = jnp.concatenate([w_rg, w_re, jnp.zeros((d_model, LANES - used), w_rg.dtype)], axis=1).astype(BF16)
    b = jnp.concatenate([b_rg, b_re, jnp.zeros((LANES - used,), b_rg.dtype)]).reshape(1, LANES)
    return w, b


def _cross(x1, qc, mk, mv, w_co, g_ffn, w_router, b_router, batch, t_len, tq):
    n, d_model = x1.shape
    tq = min(tq, t_len)
    assert t_len % tq == 0
    nq = t_len // tq
    d_cross = qc.shape[1]
    row_spec = lambda w: pl.BlockSpec((tq, w), lambda b, j: (b * nq + j, 0))
    mem_spec = pl.BlockSpec((1,) + mk.shape[1:], lambda b, j: (b, 0, 0, 0))
    return pl.pallas_call(
        _cross_kernel,
        grid=(batch, nq),
        in_specs=[row_spec(d_model), row_spec(d_cross), mem_spec, mem_spec, _const_spec(w_co.shape),
                  _const_spec((1, d_model)), _const_spec(w_router.shape), _const_spec((1, LANES))],
        out_specs=(row_spec(d_model), pl.BlockSpec((tq, d_model // LANES, LANES), lambda b, j: (b * nq + j, 0, 0)),
                   row_spec(LANES)),
        out_shape=(jax.ShapeDtypeStruct((n, d_model), F32), jax.ShapeDtypeStruct((n, d_model // LANES, LANES), F32),
                   jax.ShapeDtypeStruct((n, LANES), F32)),
        compiler_params=_cparams(("parallel", "parallel")),
        name="cross_router",
    )(x1, qc, mk, mv, w_co.astype(BF16), g_ffn.reshape(1, d_model), w_router, b_router)


def _row_gather_start(src_hbm, idx_ref, dst, sem, n_rows):
    def issue(r, carry):
        pltpu.make_async_copy(src_hbm.at[pl.ds(idx_ref[0, 0, r], 1)], dst.at[pl.ds(r, 1)], sem).start()
        return carry
    lax.fori_loop(0, n_rows, issue, 0)


def _row_gather_wait(src_hbm, dst, sem, n_rows):
    pltpu.make_async_copy(src_hbm.at[pl.ds(0, n_rows)], dst, sem).wait()


def _expert_kernel(be_ref, tok_ref, tok_next_ref, x_hbm, w1_ref, w3_ref, w2_ref, y_ref, xbuf, sems):
    del be_ref
    i = pl.program_id(0)
    nb = pl.num_programs(0)
    rows = xbuf.shape[1]
    slot = i % 2

    @pl.when(i == 0)
    def _():
        _row_gather_start(x_hbm, tok_ref, xbuf.at[0], sems.at[0], rows)

    @pl.when(i + 1 < nb)
    def _():
        _row_gather_start(x_hbm, tok_next_ref, xbuf.at[1 - slot], sems.at[1 - slot], rows)

    _row_gather_wait(x_hbm, xbuf.at[slot], sems.at[slot], rows)
    n_slab = xbuf.shape[2]
    x = jnp.concatenate([xbuf[slot, :, s, :] for s in range(n_slab)], axis=1).astype(BF16)
    a = jnp.dot(x, w1_ref[0], preferred_element_type=F32)
    b = jnp.dot(x, w3_ref[0], preferred_element_type=F32)
    h = (a * jax.nn.sigmoid(a) * b).astype(BF16)
    y = jnp.dot(h, w2_ref[0], preferred_element_type=F32)
    for s in range(n_slab):
        y_ref[:, s, :] = y[:, s * LANES:(s + 1) * LANES]


def _combine_kernel(s0_ref, s1_ref, s0n_ref, s1n_ref, yb_hbm, x2_ref, route_ref, gfin_ref, out_ref, ybuf, sems):
    i = pl.program_id(0)
    nb = pl.num_programs(0)
    rows = x2_ref.shape[0]
    slot = i % 2

    def start(a_ref, b_ref, s):
        _row_gather_start(yb_hbm, a_ref, ybuf.at[s, 0], sems.at[s, 0], rows)
        _row_gather_start(yb_hbm, b_ref, ybuf.at[s, 1], sems.at[s, 1], rows)

    @pl.when(i == 0)
    def _():
        start(s0_ref, s1_ref, 0)

    @pl.when(i + 1 < nb)
    def _():
        start(s0n_ref, s1n_ref, 1 - slot)

    _row_gather_wait(yb_hbm, ybuf.at[slot, 0], sems.at[slot, 0], rows)
    _row_gather_wait(yb_hbm, ybuf.at[slot, 1], sems.at[slot, 1], rows)
    route = route_ref[...]
    g0 = route[:, ROUTE_GATE0:ROUTE_GATE0 + 1]
    g1 = route[:, ROUTE_GATE1:ROUTE_GATE1 + 1]
    y = jnp.concatenate([g0 * ybuf[slot, 0, :, s, :] + g1 * ybuf[slot, 1, :, s, :] for s in range(ybuf.shape[3])],
                        axis=1)
    out_ref[...] = _rms(x2_ref[...] + y, gfin_ref[...])


def _moe_and_final(x2, u3, route, w_e1, w_e3, w_e2, g_final, tm, blk):
    n, d_model = x2.shape
    n_slab = d_model // LANES
    n_exp = w_e1.shape[0]
    a_total = n * EXPERT_TOPK
    n_blocks = -(-(a_total + n_exp * (blk - 1)) // blk)
    p_rows = n_blocks * blk

    fe = route[:, ROUTE_EXPERT0:ROUTE_EXPERT1 + 1].astype(I32).reshape(-1)
    onehot = (fe[:, None] == jnp.arange(n_exp, dtype=I32)[None, :]).astype(I32)
    csum = jnp.cumsum(onehot, axis=0)
    rank = jnp.sum((csum - onehot) * onehot, axis=1)
    counts = csum[-1]
    padded = (counts + blk - 1) // blk * blk
    pad_end = jnp.cumsum(padded)
    dest = (pad_end - padded)[fe] + rank
    ft = jnp.arange(a_total, dtype=I32) // EXPERT_TOPK
    slot_tok = jnp.zeros((p_rows,), I32).at[dest].set(ft)
    block_start = jnp.arange(n_blocks, dtype=I32) * blk
    block_exp = jnp.minimum(jnp.sum((pad_end[None, :] <= block_start[:, None]).astype(I32), axis=1), n_exp - 1)

    tok3 = slot_tok.reshape(n_blocks, 1, blk)
    smem_blk = lambda w, f: pl.BlockSpec((1, 1, w), f, memory_space=pltpu.SMEM)
    w_spec = lambda shp: pl.BlockSpec((1,) + shp[1:], lambda i, be: (be[i], 0, 0))
    yb = pl.pallas_call(
        _expert_kernel,
        grid_spec=pltpu.PrefetchScalarGridSpec(
            num_scalar_prefetch=1,
            grid=(n_blocks,),
            in_specs=[smem_blk(blk, lambda i, be: (i, 0, 0)),
                      smem_blk(blk, lambda i, be: (jnp.minimum(i + 1, n_blocks - 1), 0, 0)),
                      pl.BlockSpec(memory_space=pl.ANY),
                      w_spec(w_e1.shape), w_spec(w_e3.shape), w_spec(w_e2.shape)],
            out_specs=pl.BlockSpec((blk, n_slab, LANES), lambda i, be: (i, 0, 0)),
            scratch_shapes=[pltpu.VMEM((2, blk, n_slab, LANES), F32), pltpu.SemaphoreType.DMA((2,))],
        ),
        out_shape=jax.ShapeDtypeStruct((p_rows, n_slab, LANES), F32),
        compiler_params=_cparams(("arbitrary",)),
        name="moe_experts",
    )(block_exp, tok3, tok3, u3, w_e1.astype(BF16), w_e3.astype(BF16), w_e2.astype(BF16))

    tm = min(tm, n)
    assert n % tm == 0
    nt = n // tm
    d2 = dest.reshape(n, EXPERT_TOPK)
    s0 = d2[:, 0].reshape(nt, 1, tm)
    s1 = d2[:, 1].reshape(nt, 1, tm)
    cur = lambda i: (i, 0, 0)
    nxt = lambda i: (jnp.minimum(i + 1, nt - 1), 0, 0)
    smem_blk2 = lambda f: pl.BlockSpec((1, 1, tm), f, memory_space=pltpu.SMEM)
    row_spec = lambda w: pl.BlockSpec((tm, w), lambda i: (i, 0))
    return pl.pallas_call(
        _combine_kernel,
        grid=(nt,),
        in_specs=[smem_blk2(cur), smem_blk2(cur), smem_blk2(nxt), smem_blk2(nxt),
                  pl.BlockSpec(memory_space=pl.ANY), row_spec(d_model), row_spec(LANES), _const_spec((1, d_model))],
        out_specs=row_spec(d_model),
        out_shape=jax.ShapeDtypeStruct((n, d_model), F32),
        scratch_shapes=[pltpu.VMEM((2, EXPERT_TOPK, tm, n_slab, LANES), F32),
                        pltpu.SemaphoreType.DMA((2, EXPERT_TOPK))],
        compiler_params=_cparams(("arbitrary",)),
        name="moe_combine",
    )(s0, s1, s0, s1, yb, x2, route, g_final.reshape(1, d_model))


def _layer(x, pos, conv_buf, h0, attend, mk, mv, lw, g_final, tiles):
    batch, t_len, d_model = x.shape
    n = batch * t_len
    d_lru = lw['w_conv'].shape[1]
    d_att = N_HEADS * HEAD_DIM
    d_idx = IDX_HEADS * IDX_DIM
    x2d = x.reshape(n, d_model)
    xl, q, k, v, qi, kiwi, gl, ga = _in_proj(x2d, lw['g_mix'], lw['w_in_packed'], pos, t_len, d_lru, d_att, d_idx,
                                             tiles['in_proj'])
    att = attend(qi, q, kiwi, k, v)
    hs, h_last, new_buf = _rglru(xl.reshape(batch, t_len, d_lru), conv_buf, h0, lw['w_conv'], lw['b_conv'],
                                 lw['w_ra'], lw['b_ra'], lw['w_ri'], lw['b_ri'], lw['lru_lambda'],
                                 tiles['lru_t'], tiles['lru_b'])
    x1, qc = _mix(x2d, hs.reshape(n, d_lru), att, gl, ga, lw['w_lru_out'], lw['w_att_out'], lw['w_mix_out'],
                  lw['g_cross'], lw['w_cq'], tiles['mix'])
    x2, u3, route = _cross(x1, qc, mk, mv, lw['w_co'], lw['g_ffn'], lw['w_router'], lw['b_router'], batch, t_len,
                           tiles['cross'])
    y = _moe_and_final(x2, u3, route, lw['w_e1'], lw['w_e3'], lw['w_e2'], g_final, tiles['combine'],
                       tiles['moe_rows'])
    k5 = k.reshape(batch, t_len, N_HEADS, HEAD_DIM)
    v5 = v.reshape(batch, t_len, N_HEADS, HEAD_DIM)
    ki = kiwi[:, :IDX_DIM].reshape(batch, t_len, IDX_DIM)
    return y.reshape(batch, t_len, d_model), new_buf, h_last, k5, v5, ki


def kernel(x_prompt, mem_prompt, x_sample, cache_k, cache_v, cache_kidx, cache_mem_k, cache_mem_v, state_conv, state_lru, page_table, g_mix, w_in, w_conv, b_conv, w_ra, b_ra, w_ri, b_ri, lru_lambda, w_lru_out, w_att_out, w_mix_out, g_cross, g_mem, w_cq, w_mk, w_mv, w_co, g_ffn, w_rg, b_rg, w_re, b_re, w_e1, w_e3, w_e2, g_final):
    depth = w_in.shape[0]
    assert depth == 1, "the final norm is fused into the last layer's MoE combine; one layer supported"
    l = 0
    b_p, t_p, _ = x_prompt.shape
    b_s, t_s, _ = x_sample.shape
    past = page_table.shape[1] * cache_k.shape[2]
    d_lru = w_conv.shape[2]
    d_att = N_HEADS * HEAD_DIM
    d_idx = IDX_HEADS * IDX_DIM
    w_router, b_router = _pack_router(w_rg[l], b_rg[l], w_re[l], b_re[l])
    lw = {
        'g_mix': g_mix[l], 'w_in_packed': _pack_w_in(w_in[l], d_lru, d_att, d_idx),
        'w_conv': w_conv[l], 'b_conv': b_conv[l], 'w_ra': w_ra[l], 'b_ra': b_ra[l], 'w_ri': w_ri[l], 'b_ri': b_ri[l],
        'lru_lambda': lru_lambda[l], 'w_lru_out': w_lru_out[l], 'w_att_out': w_att_out[l], 'w_mix_out': w_mix_out[l],
        'g_cross': g_cross[l], 'w_cq': w_cq[l], 'w_co': w_co[l], 'g_ffn': g_ffn[l],
        'w_router': w_router, 'b_router': b_router, 'w_e1': w_e1[l], 'w_e3': w_e3[l], 'w_e2': w_e2[l],
    }

    mk_p, mv_p = _mem_kv(mem_prompt, g_mem[l], w_mk[l], w_mv[l])
    attend_p = functools.partial(_prompt_attention, batch=b_p, t_len=t_p)
    tiles_p = {'in_proj': 512, 'lru_t': 128, 'lru_b': 8, 'mix': 512, 'cross': 256, 'combine': 256, 'moe_rows': 256}
    y_p, buf_p, h_p, k_p, v_p, ki_p = _layer(
        x_prompt, jnp.arange(t_p), jnp.zeros((b_p, CONV_W - 1, d_lru), F32), jnp.zeros((b_p, d_lru), F32),
        attend_p, mk_p, mv_p, lw, g_final, tiles_p)

    def attend_s(qi, q, kiwi, k, v):
        return _sample_attention(qi, q, kiwi, k, v, cache_k, cache_v, cache_kidx, page_table, l, t_s,
                                 pg_sel=16, pg_att=16)
    tiles_s = {'in_proj': 512, 'lru_t': 8, 'lru_b': 8, 'mix': 512, 'cross': 8, 'combine': 256, 'moe_rows': 128}
    y_s, buf_s, h_s, k_s, v_s, ki_s = _layer(
        x_sample, past + jnp.arange(t_s), state_conv[l], state_lru[l], attend_s, cache_mem_k[l], cache_mem_v[l],
        lw, g_final, tiles_s)

    st = lambda a: a[None]
    return (y_p, y_s, st(k_p), st(v_p), st(ki_p), st(buf_p), st(h_p), st(mk_p), st(mv_p),
            st(k_s), st(v_s), st(ki_s), st(buf_s), st(h_s))
```

```python
import functools
import math

import jax
import jax.numpy as jnp
import numpy as np
from jax import lax
from jax.experimental import pallas as pl
from jax.experimental.pallas import tpu as pltpu

F32 = jnp.float32
BF16 = jnp.bfloat16
I32 = jnp.int32

N_HEADS = 8
HEAD_DIM = 64
IDX_HEADS = 8
IDX_DIM = 64
TOPK_MAX = 256
Q_BLOCK = 128
ROPE_THETA = 10000.0
CONV_W = 4
LRU_C = 8.0
LRU_BLOCKS = 8
C_HEADS = 4
N_GROUPS = 4
EXPERTS_PER_GROUP = 8
EXPERT_TOPK = 2
MOE_BLOCK = 128
NORM_EPS = 1e-6

LANES = 128
VMEM_LIMIT = 56 * 1024 * 1024


def _cparams(sem):
    return pltpu.CompilerParams(dimension_semantics=sem, vmem_limit_bytes=VMEM_LIMIT)


def _const_spec(shape):
    nd = len(shape)
    return pl.BlockSpec(shape, lambda *_: (0,) * nd)


def _rms(x, g):
    return x * lax.rsqrt(jnp.mean(x * x, axis=-1, keepdims=True) + NORM_EPS) * g


def _rope_rot(x, cos, sin_signed):
    w = x.shape[1]
    reps = w // LANES
    fwd = pltpu.roll(x, 32, axis=1)
    bwd = pltpu.roll(x, w - 32, axis=1)
    lane = lax.broadcasted_iota(I32, x.shape, 1)
    rot = jnp.where((lane % 64) < 32, bwd, fwd)
    if reps > 1:
        cos = jnp.concatenate([cos] * reps, axis=1)
        sin_signed = jnp.concatenate([sin_signed] * reps, axis=1)
    return x * cos + rot * sin_signed


def _in_proj_kernel(x_ref, g_ref, w_ref, cos_ref, sin_ref,
                    xl_ref, q_ref, k_ref, v_ref, qi_ref, kiwi_ref, gl_ref, ga_ref, *, d_lru, d_att, d_idx, d_model):
    u = _rms(x_ref[...], g_ref[...]).astype(BF16)
    cos = cos_ref[...]
    sin = sin_ref[...]

    def proj(c0, width):
        return jnp.dot(u, w_ref[:, c0:c0 + width], preferred_element_type=F32)

    c = 0
    xl_ref[...] = proj(c, d_lru); c += d_lru
    q_ref[...] = (_rope_rot(proj(c, d_att), cos, sin) * (HEAD_DIM ** -0.5)).astype(BF16); c += d_att
    k_ref[...] = _rope_rot(proj(c, d_att), cos, sin); c += d_att
    v_ref[...] = proj(c, d_att); c += d_att
    qi_ref[...] = (_rope_rot(proj(c, d_idx), cos, sin) * (IDX_DIM ** -0.5)).astype(BF16); c += d_idx
    kw = proj(c, LANES); c += LANES
    lane = lax.broadcasted_iota(I32, kw.shape, 1)
    kiwi_ref[...] = jnp.where(lane < IDX_DIM, _rope_rot(kw, cos, sin), kw * (IDX_HEADS ** -0.5))
    gl_ref[...] = jax.nn.sigmoid(proj(c, d_model)); c += d_model
    ga_ref[...] = jax.nn.sigmoid(proj(c, d_model))


def _rope_tables(pos, rows):
    inv = ROPE_THETA ** (-jnp.arange(0, HEAD_DIM, 2, dtype=F32) / HEAD_DIM)
    ang = pos.astype(F32)[:, None] * inv[None, :]
    cos = jnp.cos(ang)
    sin = jnp.sin(ang)
    cos_t = jnp.concatenate([cos, cos, cos, cos], axis=1)
    sin_t = jnp.concatenate([-sin, sin, -sin, sin], axis=1)
    reps = rows // pos.shape[0]
    if reps > 1:
        cos_t = jnp.tile(cos_t, (reps, 1))
        sin_t = jnp.tile(sin_t, (reps, 1))
    return cos_t, sin_t


def _in_proj(x2d, g, w_packed, pos, seq_len, d_lru, d_att, d_idx, tm):
    n, d_model = x2d.shape
    tm = min(tm, n)
    assert n % tm == 0
    rows = max(seq_len, tm)
    assert rows % tm == 0 and (tm % seq_len == 0 or seq_len % tm == 0)
    cos_t, sin_t = _rope_tables(pos, rows)
    nt = rows // tm
    row_spec = lambda w: pl.BlockSpec((tm, w), lambda i: (i, 0))
    tab_spec = pl.BlockSpec((tm, LANES), lambda i: (i % nt, 0))
    kern = functools.partial(_in_proj_kernel, d_lru=d_lru, d_att=d_att, d_idx=d_idx, d_model=d_model)
    out_shape = (
        jax.ShapeDtypeStruct((n, d_lru), F32),
        jax.ShapeDtypeStruct((n, d_att), BF16),
        jax.ShapeDtypeStruct((n, d_att), F32),
        jax.ShapeDtypeStruct((n, d_att), F32),
        jax.ShapeDtypeStruct((n, d_idx), BF16),
        jax.ShapeDtypeStruct((n, LANES), F32),
        jax.ShapeDtypeStruct((n, d_model), F32),
        jax.ShapeDtypeStruct((n, d_model), F32),
    )
    return pl.pallas_call(
        kern,
        grid=(n // tm,),
        in_specs=[row_spec(d_model), _const_spec((1, d_model)), _const_spec(w_packed.shape), tab_spec, tab_spec],
        out_specs=tuple(row_spec(s.shape[1]) for s in out_shape),
        out_shape=out_shape,
        compiler_params=_cparams(("parallel",)),
        name="in_proj",
    )(x2d, g.reshape(1, d_model), w_packed, cos_t, sin_t)


def _pack_w_in(w_in, d_lru, d_att, d_idx):
    d_model = w_in.shape[0]
    c = d_lru + 3 * d_att + d_idx
    kiwi = w_in[:, c:c + IDX_DIM + IDX_HEADS]
    pad = jnp.zeros((d_model, LANES - IDX_DIM - IDX_HEADS), w_in.dtype)
    return jnp.concatenate([w_in[:, :c], kiwi, pad, w_in[:, c + IDX_DIM + IDX_HEADS:]], axis=1).astype(BF16)


CONV_PAD = 8


def _softplus(x):
    return jnp.maximum(x, 0.0) + jnp.log1p(jnp.exp(-jnp.abs(x)))


def _rglru_kernel(xl_ref, buf0_ref, h0_ref, wconv_ref, bconv_ref, wra_ref, bra_ref, wri_ref, bri_ref, lam_ref,
                  hs_ref, hlast_ref, newbuf_ref, xpad_scr, a_scr, u_scr, h_scr, *, tc, bb):
    j = pl.program_id(1)
    d = xl_ref.shape[-1]
    hist = CONV_W - 1

    @pl.when(j == 0)
    def _():
        xpad_scr[:, 0:CONV_PAD - hist, :] = jnp.zeros((bb, CONV_PAD - hist, d), F32)
        xpad_scr[:, CONV_PAD - hist:CONV_PAD, :] = buf0_ref[...]
        h_scr[...] = h0_ref[...]

    xpad_scr[:, CONV_PAD:CONV_PAD + tc, :] = xl_ref[...]
    xc = jnp.zeros((bb, tc, d), F32) + bconv_ref[...]
    for i in range(CONV_W):
        off = CONV_PAD - hist + i
        xc = xc + xpad_scr[:, off:off + tc, :] * wconv_ref[i:i + 1, :]
    newbuf_ref[...] = xpad_scr[:, CONV_PAD + tc - hist:CONV_PAD + tc, :]
    xpad_scr[:, 0:CONV_PAD, :] = xpad_scr[:, tc:tc + CONV_PAD, :]

    xc2 = xc.reshape(bb * tc, d)
    xb = xc2.astype(BF16)
    r = jax.nn.sigmoid(jnp.dot(xb, wra_ref[...], preferred_element_type=F32) + bra_ref[...])
    g = jax.nn.sigmoid(jnp.dot(xb, wri_ref[...], preferred_element_type=F32) + bri_ref[...])
    log_a = (-LRU_C) * r * _softplus(-lam_ref[...])
    a = jnp.exp(log_a)
    mult = jnp.sqrt(-jnp.tanh(log_a) * (a * a + 1.0))
    a_scr[...] = a.reshape(bb, tc, d)
    u_scr[...] = (mult * (g * xc2)).reshape(bb, tc, d)

    def step(t, hs):
        new = []
        for b in range(bb):
            h = a_scr[b, pl.ds(t, 1), :] * hs[b] + u_scr[b, pl.ds(t, 1), :]
            u_scr[b, pl.ds(t, 1), :] = h
            new.append(h)
        return tuple(new)

    h_fin = lax.fori_loop(0, tc, step, tuple(h_scr[b] for b in range(bb)))
    for b in range(bb):
        h_scr[b] = h_fin[b]
        hlast_ref[b] = h_fin[b]
    hs_ref[...] = u_scr[...].astype(hs_ref.dtype)


def _block_diag(w):
    nb, c, _ = w.shape
    eye = jnp.eye(nb, dtype=w.dtype)
    return (eye[:, None, :, None] * w[:, :, None, :]).reshape(nb * c, nb * c)


def _rglru(xl, buf0, h0, w_conv, b_conv, w_ra, b_ra, w_ri, b_ri, lam, tc, bb):
    b, t, d = xl.shape
    tc = min(tc, t)
    bb = min(bb, b)
    assert t % tc == 0 and b % bb == 0 and tc % 8 == 0
    row = lambda a: a.reshape(1, d)
    kern = functools.partial(_rglru_kernel, tc=tc, bb=bb)
    hist = CONV_W - 1
    out_shape = (
        jax.ShapeDtypeStruct((b, t, d), BF16 if tc % 16 == 0 else F32),
        jax.ShapeDtypeStruct((b, 1, d), F32),
        jax.ShapeDtypeStruct((b, hist, d), F32),
    )
    hs, h_last, new_buf = pl.pallas_call(
        kern,
        grid=(b // bb, t // tc),
        in_specs=[
            pl.BlockSpec((bb, tc, d), lambda i, j: (i, j, 0)),
            pl.BlockSpec((bb, hist, d), lambda i, j: (i, 0, 0)),
            pl.BlockSpec((bb, 1, d), lambda i, j: (i, 0, 0)),
            _const_spec((CONV_W, d)), _const_spec((1, d)),
            _const_spec((d, d)), _const_spec((1, d)), _const_spec((d, d)), _const_spec((1, d)), _const_spec((1, d)),
        ],
        out_specs=(
            pl.BlockSpec((bb, tc, d), lambda i, j: (i, j, 0)),
            pl.BlockSpec((bb, 1, d), lambda i, j: (i, 0, 0)),
            pl.BlockSpec((bb, hist, d), lambda i, j: (i, 0, 0)),
        ),
        out_shape=out_shape,
        scratch_shapes=[
            pltpu.VMEM((bb, tc + CONV_PAD, d), F32),
            pltpu.VMEM((bb, tc, d), F32),
            pltpu.VMEM((bb, tc, d), F32),
            pltpu.VMEM((bb, 1, d), F32),
        ],
        compiler_params=_cparams(("parallel", "arbitrary")),
        name="rglru",
    )(xl, buf0, h0.reshape(b, 1, d), w_conv, row(b_conv),
      _block_diag(w_ra).astype(BF16), row(b_ra), _block_diag(w_ri).astype(BF16), row(b_ri), row(lam))
    return hs, h_last.reshape(b, d), new_buf


INT_MIN = -2 ** 31
NEG_INF_KEY = -2139095041
KEY_GROUP = 4
SELECT_SEQS = 8


def _sortable_key(score):
    score = jnp.where(score == 0.0, 0.0, score)
    bits = pltpu.bitcast(score, I32)
    return bits ^ ((bits >> 31) & 0x7FFFFFFF)


def _pair_block_diag(x):
    lane = lax.broadcasted_iota(I32, x.shape, 1)
    zero = jnp.zeros_like(x)
    return jnp.concatenate([jnp.where(lane < 64, x, zero), jnp.where(lane >= 64, x, zero)], axis=0)


def _dot_nt(a, b):
    return lax.dot_general(a, b, (((1,), (1,)), ((), ())), preferred_element_type=F32)


def _selection_bias(kk, eq, tie_rank, theta, need):
    take = jnp.where(eq, tie_rank, jnp.inf) <= need
    b = jnp.where(take, 0.0, -jnp.inf)
    b = jnp.where(kk > theta, 0.0, b)
    return jnp.where(kk == NEG_INF_KEY, -jnp.inf, b)


def _prompt_attn_kernel(qi_ref, q_ref, kiwi_ref, k_ref, v_ref, att_ref,
                        kdup_scr, kbf_scr, vT_scr, qibd_scr, qbd_scr, key_scr, bias_scr, oT_scr, acc_scr,
                        *, topk, qb):
    j = pl.program_id(1)
    t_len = k_ref.shape[0]
    n_pairs = q_ref.shape[1] // LANES
    nk = j + 1

    def rows_of(c):
        return pl.ds(pl.multiple_of(c * qb, qb), qb)

    @pl.when(j == 0)
    def _():
        def prep(c, carry):
            rows = rows_of(c)
            kw = kiwi_ref[rows, :]
            lane = lax.broadcasted_iota(I32, kw.shape, 1)
            kdup_scr[rows, :] = jnp.where(lane < IDX_DIM, kw, pltpu.roll(kw, IDX_DIM, axis=1)).astype(BF16)
            kbf_scr[rows, :] = k_ref[rows, :].astype(BF16)
            vT_scr[:, rows] = v_ref[rows, :].T.astype(BF16)
            return carry
        lax.fori_loop(0, t_len // qb, prep, 0)

    for p in range(n_pairs):
        qibd_scr[p] = _pair_block_diag(qi_ref[:, p * LANES:(p + 1) * LANES])
        qbd_scr[p] = _pair_block_diag(q_ref[:, p * LANES:(p + 1) * LANES])
    w_t = kiwi_ref[rows_of(j), :].T

    def key_loop(body, init):
        wide = KEY_GROUP * qb
        n_groups = (nk + KEY_GROUP - 1) // KEY_GROUP
        return lax.fori_loop(0, n_groups, lambda i, c: body(pl.multiple_of(i * wide, wide), wide, c), init)

    def idx_body(r0, nr, carry):
        rows = pl.ds(r0, nr)
        kd = kdup_scr[rows, :]
        sc = jnp.zeros((nr, qb), F32)
        for p in range(n_pairs):
            s2 = jnp.maximum(_dot_nt(kd, qibd_scr[p]), 0.0)
            h = IDX_DIM + 2 * p
            sc = sc + w_t[h:h + 1, :] * s2[:, :qb] + w_t[h + 1:h + 2, :] * s2[:, qb:]
        kpos = r0 + lax.broadcasted_iota(I32, (nr, qb), 0)
        qpos = j * qb + lax.broadcasted_iota(I32, (nr, qb), 1)
        key_scr[rows, :] = _sortable_key(jnp.where(kpos <= qpos, sc, -jnp.inf))
        return carry
    key_loop(idx_body, 0)

    def count(pred):
        def body(r0, nr, acc):
            hit = jnp.where(pred(key_scr[pl.ds(r0, nr), :]), 1, 0).astype(I32)
            return acc + jnp.sum(hit.reshape(nr // 8, 8, qb), axis=0)
        return jnp.sum(key_loop(body, jnp.zeros((8, qb), I32)), axis=0, keepdims=True)

    def bit_step(i, prefix):
        cand_u = prefix | jnp.left_shift(jnp.int32(1), 31 - i)
        cand_s = cand_u ^ INT_MIN
        return jnp.where(count(lambda kk: kk >= cand_s) >= topk, cand_u, prefix)
    theta = lax.fori_loop(0, 32, bit_step, jnp.zeros((1, qb), I32)) ^ INT_MIN

    need = (topk - count(lambda kk: kk > theta)).astype(F32)
    tri = jnp.where(lax.broadcasted_iota(I32, (qb, qb), 0) >= lax.broadcasted_iota(I32, (qb, qb), 1),
                    1.0, 0.0).astype(BF16)

    def mask_body(r0, nr, run):
        for s in range(nr // qb):
            rows = pl.ds(r0 + s * qb, qb)
            kk = key_scr[rows, :]
            eq = kk == theta
            pre = jnp.dot(tri, jnp.where(eq, 1.0, 0.0).astype(BF16), preferred_element_type=F32)
            bias_scr[rows, :] = _selection_bias(kk, eq, run + pre, theta, need)
            run = run + pre[qb - 1:qb, :]
        return run
    key_loop(mask_body, jnp.zeros((1, qb), F32))

    def pair_lanes(p):
        return slice(p * LANES, (p + 1) * LANES)

    acc_scr[...] = jnp.zeros(acc_scr.shape, F32)

    def attend(r0, nr, carry):
        ms, l8s = carry
        rows = pl.ds(r0, nr)
        bias = bias_scr[rows, :]
        bias2 = jnp.concatenate([bias, bias], axis=1)
        new_m, new_l = [], []
        for p in range(n_pairs):
            lg = _dot_nt(kbf_scr[rows, pair_lanes(p)], qbd_scr[p]) + bias2
            m_blk = jnp.max(jnp.max(lg.reshape(nr // 8, 8, 2 * qb), axis=0), axis=0, keepdims=True)
            m_new = jnp.maximum(ms[p], m_blk)
            m_safe = jnp.where(m_new == -jnp.inf, 0.0, m_new)
            alpha = jnp.exp(ms[p] - m_safe)
            pe = jnp.exp(lg - m_safe)
            new_l.append(alpha * l8s[p] + jnp.sum(pe.reshape(nr // 8, 8, 2 * qb), axis=0))
            acc_scr[p] = alpha * acc_scr[p] + jnp.dot(vT_scr[pair_lanes(p), rows], pe.astype(BF16),
                                                      preferred_element_type=F32)
            new_m.append(m_new)
        return tuple(new_m), tuple(new_l)
    _, l8s = key_loop(attend, (tuple(jnp.full((1, 2 * qb), -jnp.inf, F32) for _ in range(n_pairs)),
                               tuple(jnp.zeros((8, 2 * qb), F32) for _ in range(n_pairs))))

    for p in range(n_pairs):
        o = acc_scr[p] / jnp.sum(l8s[p], axis=0, keepdims=True)
        oT_scr[p * LANES:p * LANES + HEAD_DIM, :] = o[:HEAD_DIM, :qb]
        oT_scr[p * LANES + HEAD_DIM:(p + 1) * LANES, :] = o[HEAD_DIM:, qb:]
    att_ref[...] = oT_scr[...].T.astype(att_ref.dtype)


def _prompt_attention(qi, q, kiwi, k, v, batch, t_len):
    n, d_att = q.shape
    qb = min(Q_BLOCK, t_len)
    assert qb == LANES and t_len % (KEY_GROUP * qb) == 0
    nqb = t_len // qb
    topk = min(TOPK_MAX, t_len // 4)
    n_pairs = d_att // LANES
    kern = functools.partial(_prompt_attn_kernel, topk=topk, qb=qb)
    blk_spec = lambda w: pl.BlockSpec((qb, w), lambda b, j: (b * nqb + j, 0))
    seq_spec = lambda w: pl.BlockSpec((t_len, w), lambda b, j: (b, 0))
    return pl.pallas_call(
        kern,
        grid=(batch, nqb),
        in_specs=[blk_spec(qi.shape[1]), blk_spec(d_att), seq_spec(LANES), seq_spec(d_att), seq_spec(d_att)],
        out_specs=blk_spec(d_att),
        out_shape=jax.ShapeDtypeStruct((n, d_att), BF16),
        scratch_shapes=[
            pltpu.VMEM((t_len, LANES), BF16),
            pltpu.VMEM((t_len, d_att), BF16),
            pltpu.VMEM((d_att, t_len), BF16),
            pltpu.VMEM((n_pairs, 2 * qb, LANES), BF16),
            pltpu.VMEM((n_pairs, 2 * qb, LANES), BF16),
            pltpu.VMEM((t_len, qb), I32),
            pltpu.VMEM((t_len, qb), F32),
            pltpu.VMEM((d_att, qb), F32),
            pltpu.VMEM((n_pairs, LANES, 2 * qb), F32),
        ],
        compiler_params=_cparams(("parallel", "arbitrary")),
        name="prompt_attn",
    )(qi, q, kiwi, k, v)


def _sample_score_kernel(pt_ref, qi_ref, wi_ref, kinew_ref, *rest, pg, past, page, t_new):
    del pt_ref
    page_refs = rest[:pg]
    key_ref = rest[pg]
    g = pl.program_id(1)
    qi = qi_ref[...]
    wi = wi_ref[...]

    def head_sum(s):
        s = jnp.maximum(s, 0.0) * wi
        return jnp.sum(s.reshape(IDX_HEADS, t_new, s.shape[1]), axis=0)

    k_t = jnp.concatenate([r[0, 0] for r in page_refs], axis=1).astype(BF16)
    off = pl.multiple_of(g * (pg * page), page)
    key_ref[0, :, pl.ds(off, pg * page)] = _sortable_key(head_sum(jnp.dot(qi, k_t, preferred_element_type=F32)))

    @pl.when(g == pl.num_programs(1) - 1)
    def _():
        knew = jnp.concatenate([kinew_ref[:, :IDX_DIM], jnp.zeros((page - t_new, IDX_DIM), F32)], axis=0)
        sc = head_sum(_dot_nt(qi, knew.astype(BF16)))
        tok = lax.broadcasted_iota(I32, sc.shape, 0)
        kk_i = lax.broadcasted_iota(I32, sc.shape, 1)
        key_ref[0, :, past:past + page] = _sortable_key(jnp.where(kk_i <= tok, sc, -jnp.inf))


def _sample_threshold_kernel(key_ref, bias_ref, *, topk, page):
    n_seq, t_new, l_pad = key_ref.shape
    rows = n_seq * t_new
    n_chunks = l_pad // page

    def chunk(c):
        return key_ref[:, :, c * page:(c + 1) * page].reshape(rows, page)

    def count(pred):
        acc = jnp.zeros((rows, page), I32)
        for c in range(n_chunks):
            acc = acc + jnp.where(pred(chunk(c)), 1, 0).astype(I32)
        return jnp.sum(acc, axis=1, keepdims=True)

    def bit_step(i, prefix):
        cand_u = prefix | jnp.left_shift(jnp.int32(1), 31 - i)
        cand_s = cand_u ^ INT_MIN
        return jnp.where(count(lambda kk: kk >= cand_s) >= topk, cand_u, prefix)
    theta = lax.fori_loop(0, 32, bit_step, jnp.zeros((rows, 1), I32)) ^ INT_MIN
    need = (topk - count(lambda kk: kk > theta)).astype(F32)

    row_i = lax.broadcasted_iota(I32, (page, page), 0)
    col_i = lax.broadcasted_iota(I32, (page, page), 1)
    triu = jnp.where(row_i <= col_i, 1.0, 0.0).astype(BF16)
    run = jnp.zeros((rows, 1), F32)
    for c in range(n_chunks):
        kk = chunk(c)
        eq = kk == theta
        pre = jnp.dot(jnp.where(eq, 1.0, 0.0).astype(BF16), triu, preferred_element_type=F32)
        bias = _selection_bias(kk, eq, run + pre, theta, need)
        bias_ref[:, :, c * page:(c + 1) * page] = bias.reshape(n_seq, t_new, page)
        run = run + pre[:, page - 1:page]


def _sample_attn_kernel(pt_ref, q_ref, bias_ref, knew_ref, vnew_ref, *rest, pg, past, page, t_new):
    del pt_ref
    k_refs = rest[:pg]
    v_refs = rest[pg:2 * pg]
    out_ref = rest[2 * pg]
    m_scr, l_scr, acc_scr = rest[2 * pg + 1:]
    g = pl.program_id(1)
    qbd = q_ref[...]

    @pl.when(g == 0)
    def _():
        m_scr[...] = jnp.full(m_scr.shape, -jnp.inf, F32)
        l_scr[...] = jnp.zeros(l_scr.shape, F32)
        acc_scr[...] = jnp.zeros(acc_scr.shape, F32)

    def attend(lg, bias, pv_fn):
        lg = lg + jnp.concatenate([bias] * N_HEADS, axis=0)
        m_old = m_scr[...]
        m_new = jnp.maximum(m_old, jnp.max(lg, axis=1, keepdims=True))
        m_safe = jnp.where(m_new == -jnp.inf, 0.0, m_new)
        alpha = jnp.exp(m_old - m_safe)
        pe = jnp.exp(lg - m_safe)
        l_scr[...] = alpha * l_scr[...] + jnp.sum(pe, axis=1, keepdims=True)
        acc_scr[...] = alpha * acc_scr[...] + pv_fn(pe.astype(BF16))
        m_scr[...] = m_new

    k_t = jnp.concatenate([r[0, 0] for r in k_refs], axis=1).astype(BF16)
    v_t = jnp.concatenate([r[0, 0] for r in v_refs], axis=1).astype(BF16)
    off = pl.multiple_of(g * (pg * page), page)
    attend(jnp.dot(qbd, k_t, preferred_element_type=F32), bias_ref[0, :, pl.ds(off, pg * page)],
           lambda pb: _dot_nt(pb, v_t))

    @pl.when(g == pl.num_programs(1) - 1)
    def _():
        pad = jnp.zeros((page - t_new, knew_ref.shape[1]), F32)
        k_new = jnp.concatenate([knew_ref[...], pad], axis=0).astype(BF16)
        v_new = jnp.concatenate([vnew_ref[...], pad], axis=0).astype(BF16)
        attend(_dot_nt(qbd, k_new), bias_ref[0, :, past:past + page],
               lambda pb: jnp.dot(pb, v_new, preferred_element_type=F32))
        out_ref[0] = acc_scr[...] / l_scr[...]


def _to_head_major(x, batch, t_new, heads, dim):
    return x.reshape(batch, t_new, heads, dim).transpose(0, 2, 1, 3).reshape(batch * heads * t_new, dim)


def _sample_attention(qi, q, kiwi, k_new, v_new, cache_k, cache_v, cache_kidx, page_table, layer, t_new, pg_sel, pg_att):
    batch, n_pages = page_table.shape
    depth, n_pool, page = cache_k.shape[:3]
    past = n_pages * page
    pg_sel = min(pg_sel, n_pages)
    pg_att = min(pg_att, n_pages)
    assert n_pages % pg_sel == 0 and n_pages % pg_att == 0 and page == LANES and t_new == 8
    l_pad = past + page
    topk = min(TOPK_MAX, (past + t_new) // 4)
    rows = IDX_HEADS * t_new
    d_att = N_HEADS * HEAD_DIM

    kidx_t = cache_kidx.transpose(0, 1, 3, 2)
    k_t = cache_k.transpose(0, 1, 3, 4, 2).reshape(depth, n_pool, d_att, page)
    v_t = cache_v.transpose(0, 1, 3, 4, 2).reshape(depth, n_pool, d_att, page)

    qi_hm = _to_head_major(qi, batch, t_new, IDX_HEADS, IDX_DIM)
    wi_col = kiwi[:, IDX_DIM:IDX_DIM + IDX_HEADS].reshape(batch, t_new, IDX_HEADS).transpose(0, 2, 1).reshape(-1, 1)
    seq_spec = lambda r, w: pl.BlockSpec((r, w), lambda b, g, pt: (b, 0))
    bias_spec = pl.BlockSpec((1, t_new, l_pad), lambda b, g, pt: (b, 0, 0))

    def page_spec(arr, i, pg):
        return pl.BlockSpec((1, 1) + arr.shape[2:], lambda b, g, pt: (layer, pt[b, g * pg + i], 0, 0))

    keys = pl.pallas_call(
        functools.partial(_sample_score_kernel, pg=pg_sel, past=past, page=page, t_new=t_new),
        grid_spec=pltpu.PrefetchScalarGridSpec(
            num_scalar_prefetch=1,
            grid=(batch, n_pages // pg_sel),
            in_specs=[seq_spec(rows, IDX_DIM), seq_spec(rows, 1), seq_spec(t_new, LANES)]
                     + [page_spec(kidx_t, i, pg_sel) for i in range(pg_sel)],
            out_specs=bias_spec,
        ),
        out_shape=jax.ShapeDtypeStruct((batch, t_new, l_pad), I32),
        compiler_params=_cparams(("parallel", "arbitrary")),
        name="sample_score",
    )(page_table, qi_hm, wi_col, kiwi, *([kidx_t] * pg_sel))

    sel_seqs = min(SELECT_SEQS, batch)
    assert batch % sel_seqs == 0
    sel_spec = pl.BlockSpec((sel_seqs, t_new, l_pad), lambda b: (b, 0, 0))
    bias = pl.pallas_call(
        functools.partial(_sample_threshold_kernel, topk=topk, page=page),
        grid=(batch // sel_seqs,),
        in_specs=[sel_spec],
        out_specs=sel_spec,
        out_shape=jax.ShapeDtypeStruct((batch, t_new, l_pad), F32),
        compiler_params=_cparams(("parallel",)),
        name="sample_select",
    )(keys)

    q4 = q.reshape(batch, t_new, N_HEADS, HEAD_DIM)
    eye = jnp.eye(N_HEADS, dtype=q.dtype)
    qbd = (q4.transpose(0, 2, 1, 3)[:, :, :, None, :] * eye[None, :, None, :, None]).reshape(batch * rows, d_att)
    out = pl.pallas_call(
        functools.partial(_sample_attn_kernel, pg=pg_att, past=past, page=page, t_new=t_new),
        grid_spec=pltpu.PrefetchScalarGridSpec(
            num_scalar_prefetch=1,
            grid=(batch, n_pages // pg_att),
            in_specs=[seq_spec(rows, d_att), bias_spec, seq_spec(t_new, d_att), seq_spec(t_new, d_att)]
                     + [page_spec(k_t, i, pg_att) for i in range(pg_att)]
                     + [page_spec(v_t, i, pg_att) for i in range(pg_att)],
            out_specs=pl.BlockSpec((1, rows, d_att), lambda b, g, pt: (b, 0, 0)),
            scratch_shapes=[pltpu.VMEM((rows, 1), F32), pltpu.VMEM((rows, 1), F32), pltpu.VMEM((rows, d_att), F32)],
        ),
        out_shape=jax.ShapeDtypeStruct((batch, rows, d_att), F32),
        compiler_params=_cparams(("parallel", "arbitrary")),
        name="sample_attn",
    )(page_table, qbd, bias, k_new, v_new, *([k_t] * pg_att), *([v_t] * pg_att))
    o5 = out.reshape(batch, N_HEADS, t_new, N_HEADS, HEAD_DIM)
    att = jnp.stack([o5[:, h, :, h, :] for h in range(N_HEADS)], axis=2)
    return att.reshape(batch * t_new, d_att).astype(BF16)


def _mix_kernel(x_ref, hs_ref, att_ref, gl_ref, ga_ref, wl_ref, wa_ref, wm_ref, gc_ref, wcq_ref, x1_ref, qc_ref):
    lru = jnp.dot(hs_ref[...].astype(BF16), wl_ref[...], preferred_element_type=F32)
    att = jnp.dot(att_ref[...], wa_ref[...], preferred_element_type=F32)
    mixed = gl_ref[...] * lru + ga_ref[...] * att
    x1 = x_ref[...] + jnp.dot(mixed.astype(BF16), wm_ref[...], preferred_element_type=F32)
    x1_ref[...] = x1
    qc_ref[...] = jnp.dot(_rms(x1, gc_ref[...]).astype(BF16), wcq_ref[...], preferred_element_type=F32)


def _mix(x2d, hs, att, gl, ga, w_lru_out, w_att_out, w_mix_out, g_cross, w_cq, tm):
    n, d_model = x2d.shape
    tm = min(tm, n)
    assert n % tm == 0
    d_cross = w_cq.shape[1]
    row_spec = lambda w: pl.BlockSpec((tm, w), lambda i: (i, 0))
    return pl.pallas_call(
        _mix_kernel,
        grid=(n // tm,),
        in_specs=[row_spec(d_model), row_spec(hs.shape[1]), row_spec(att.shape[1]), row_spec(d_model), row_spec(d_model),
                  _const_spec(w_lru_out.shape), _const_spec(w_att_out.shape), _const_spec(w_mix_out.shape),
                  _const_spec((1, d_model)), _const_spec(w_cq.shape)],
        out_specs=(row_spec(d_model), row_spec(d_cross)),
        out_shape=(jax.ShapeDtypeStruct((n, d_model), F32), jax.ShapeDtypeStruct((n, d_cross), F32)),
        compiler_params=_cparams(("parallel",)),
        name="mix_out",
    )(x2d, hs, att, gl, ga, w_lru_out.astype(BF16), w_att_out.astype(BF16), w_mix_out.astype(BF16),
      g_cross.reshape(1, d_model), w_cq.astype(BF16))


def _mem_kv_kernel(mem_ref, g_ref, wk_ref, wv_ref, mk_ref, mv_ref, mkb_ref, mvb_ref):
    m = _rms(mem_ref[0], g_ref[...]).astype(BF16)
    mk = jnp.dot(m, wk_ref[...], preferred_element_type=F32)
    mv = jnp.dot(m, wv_ref[...], preferred_element_type=F32)
    hd = mk_ref.shape[-1]
    for h in range(C_HEADS):
        mk_ref[0, :, h, :] = mk[:, h * hd:(h + 1) * hd]
        mv_ref[0, :, h, :] = mv[:, h * hd:(h + 1) * hd]
    mkb_ref[0] = mk.astype(BF16)
    mvb_ref[0] = mv.astype(BF16)


def _mem_kv(mem, g_mem, w_mk, w_mv):
    b, s, d_model = mem.shape
    d_cross = w_mk.shape[1]
    hd = d_cross // C_HEADS
    kv_spec = pl.BlockSpec((1, s, C_HEADS, hd), lambda i: (i, 0, 0, 0))
    kv_shape = jax.ShapeDtypeStruct((b, s, C_HEADS, hd), F32)
    flat_spec = pl.BlockSpec((1, s, d_cross), lambda i: (i, 0, 0))
    flat_shape = jax.ShapeDtypeStruct((b, s, d_cross), BF16)
    return pl.pallas_call(
        _mem_kv_kernel,
        grid=(b,),
        in_specs=[pl.BlockSpec((1, s, d_model), lambda i: (i, 0, 0)), _const_spec((1, d_model)),
                  _const_spec(w_mk.shape), _const_spec(w_mv.shape)],
        out_specs=(kv_spec, kv_spec, flat_spec, flat_spec),
        out_shape=(kv_shape, kv_shape, flat_shape, flat_shape),
        compiler_params=_cparams(("parallel",)),
        name="mem_kv",
    )(mem, g_mem.reshape(1, d_model), w_mk.astype(BF16), w_mv.astype(BF16))


ROUTE_EXPERT0, ROUTE_EXPERT1, ROUTE_GATE0, ROUTE_GATE1 = 0, 1, 2, 3


def _lane_min_where(cond, lane):
    return jnp.min(jnp.where(cond, lane, LANES), axis=1, keepdims=True)


def _cross_kernel(x1_ref, qc_ref, mk_ref, mv_ref, wco_ref, gf_ref, wr_ref, br_ref, x2_ref, u3_ref, route_ref):
    n_seq = mk_ref.shape[0]
    tq = x1_ref.shape[0] // n_seq
    hd = mk_ref.shape[-1] // C_HEADS
    scale = hd ** -0.5
    seq_outs = []
    for sq in range(n_seq):
        rows = slice(sq * tq, (sq + 1) * tq)
        outs = []
        for h in range(C_HEADS):
            cols = slice(h * hd, (h + 1) * hd)
            s = _dot_nt(qc_ref[rows, cols].astype(BF16), mk_ref[sq, :, cols]) * scale
            pe = jnp.exp(s - jnp.max(s, axis=1, keepdims=True))
            o = jnp.dot(pe.astype(BF16), mv_ref[sq, :, cols], preferred_element_type=F32)
            outs.append(o / jnp.sum(pe, axis=1, keepdims=True))
        seq_outs.append(jnp.concatenate(outs, axis=1))
    o = jnp.concatenate(seq_outs, axis=0).astype(BF16)
    x2 = x1_ref[...] + jnp.dot(o, wco_ref[...], preferred_element_type=F32)
    x2_ref[...] = x2
    u3 = _rms(x2, gf_ref[...])
    for s in range(u3_ref.shape[1]):
        u3_ref[:, s, :] = u3[:, s * LANES:(s + 1) * LANES]

    logits = jnp.dot(u3.astype(BF16), wr_ref[...], preferred_element_type=F32) + br_ref[...]
    lane = lax.broadcasted_iota(I32, logits.shape, 1)
    lg = jnp.where(lane < N_GROUPS, logits, -jnp.inf)
    g_max = jnp.max(lg, axis=1, keepdims=True)
    g_sel = _lane_min_where(lg == g_max, lane)
    p_grp = 1.0 / jnp.sum(jnp.exp(lg - g_max), axis=1, keepdims=True)
    e_lo = N_GROUPS + g_sel * EXPERTS_PER_GROUP
    le = jnp.where((lane >= e_lo) & (lane < e_lo + EXPERTS_PER_GROUP), logits, -jnp.inf)
    m1 = jnp.max(le, axis=1, keepdims=True)
    i1 = _lane_min_where(le == m1, lane)
    le2 = jnp.where(lane == i1, -jnp.inf, le)
    m2 = jnp.max(le2, axis=1, keepdims=True)
    i2 = _lane_min_where(le2 == m2, lane)
    e2 = jnp.exp(m2 - m1)
    inv = p_grp / (1.0 + e2)
    route = jnp.where(lane == ROUTE_EXPERT0, (i1 - N_GROUPS).astype(F32), 0.0)
    route = jnp.where(lane == ROUTE_EXPERT1, (i2 - N_GROUPS).astype(F32), route)
    route = jnp.where(lane == ROUTE_GATE0, inv, route)
    route_ref[...] = jnp.where(lane == ROUTE_GATE1, inv * e2, route)


def _pack_router(w_rg, b_rg, w_re, b_re):
    d_model = w_rg.shape[0]
    used = w_rg.shape[1] + w_re.shape[1]
    w = jnp.concatenate([w_rg, w_re, jnp.zeros((d_model, LANES - used), w_rg.dtype)], axis=1).astype(BF16)
    b = jnp.concatenate([b_rg, b_re, jnp.zeros((LANES - used,), b_rg.dtype)]).reshape(1, LANES)
    return w, b


def _cross(x1, qc, mk, mv, w_co, g_ffn, w_router, b_router, batch, t_len, tq, n_seq):
    n, d_model = x1.shape
    tq = min(tq, t_len)
    n_seq = min(n_seq, batch)
    assert t_len % tq == 0 and batch % n_seq == 0 and (n_seq == 1 or tq == t_len)
    nq = t_len // tq
    d_cross = qc.shape[1]
    rows = n_seq * tq
    row_spec = lambda w: pl.BlockSpec((rows, w), lambda b, j: (b * nq + j, 0))
    mem_spec = pl.BlockSpec((n_seq,) + mk.shape[1:], lambda b, j: (b, 0, 0))
    return pl.pallas_call(
        _cross_kernel,
        grid=(batch // n_seq, nq),
        in_specs=[row_spec(d_model), row_spec(d_cross), mem_spec, mem_spec, _const_spec(w_co.shape),
                  _const_spec((1, d_model)), _const_spec(w_router.shape), _const_spec((1, LANES))],
        out_specs=(row_spec(d_model), pl.BlockSpec((rows, d_model // LANES, LANES), lambda b, j: (b * nq + j, 0, 0)),
                   row_spec(LANES)),
        out_shape=(jax.ShapeDtypeStruct((n, d_model), F32), jax.ShapeDtypeStruct((n, d_model // LANES, LANES), F32),
                   jax.ShapeDtypeStruct((n, LANES), F32)),
        compiler_params=_cparams(("parallel", "parallel")),
        name="cross_router",
    )(x1, qc, mk, mv, w_co.astype(BF16), g_ffn.reshape(1, d_model), w_router, b_router)


DMA_PRIORITIES = 2


def _row_gather_start(src_hbm, idx_ref, dst, sem, n_rows):
    def issue(i, carry):
        for u in range(DMA_PRIORITIES):
            r = i * DMA_PRIORITIES + u
            pltpu.make_async_copy(src_hbm.at[pl.ds(idx_ref[0, 0, r], 1)], dst.at[pl.ds(r, 1)], sem).start(priority=u)
        return carry
    lax.fori_loop(0, n_rows // DMA_PRIORITIES, issue, 0)


def _row_gather_wait(src_hbm, dst, sem, n_rows):
    pltpu.make_async_copy(src_hbm.at[pl.ds(0, n_rows)], dst, sem).wait()


def _expert_kernel(be_ref, tok_ref, tok_next_ref, x_hbm, w1_ref, w3_ref, w2_ref, y_ref, xbuf, sems):
    del be_ref
    i = pl.program_id(0)
    nb = pl.num_programs(0)
    rows = xbuf.shape[1]
    slot = i % 2

    @pl.when(i == 0)
    def _():
        _row_gather_start(x_hbm, tok_ref, xbuf.at[0], sems.at[0], rows)

    @pl.when(i + 1 < nb)
    def _():
        _row_gather_start(x_hbm, tok_next_ref, xbuf.at[1 - slot], sems.at[1 - slot], rows)

    _row_gather_wait(x_hbm, xbuf.at[slot], sems.at[slot], rows)
    n_slab = xbuf.shape[2]
    x = jnp.concatenate([xbuf[slot, :, s, :] for s in range(n_slab)], axis=1).astype(BF16)
    a = jnp.dot(x, w1_ref[0], preferred_element_type=F32)
    b = jnp.dot(x, w3_ref[0], preferred_element_type=F32)
    h = (a * jax.nn.sigmoid(a) * b).astype(BF16)
    y = jnp.dot(h, w2_ref[0], preferred_element_type=F32)
    for s in range(n_slab):
        y_ref[:, s, :] = y[:, s * LANES:(s + 1) * LANES]


def _combine_kernel(s0_ref, s1_ref, s0n_ref, s1n_ref, yb_hbm, x2_ref, route_ref, gfin_ref, out_ref, ybuf, sems):
    i = pl.program_id(0)
    nb = pl.num_programs(0)
    rows = x2_ref.shape[0]
    slot = i % 2

    def start(a_ref, b_ref, s):
        _row_gather_start(yb_hbm, a_ref, ybuf.at[s, 0], sems.at[s, 0], rows)
        _row_gather_start(yb_hbm, b_ref, ybuf.at[s, 1], sems.at[s, 1], rows)

    @pl.when(i == 0)
    def _():
        start(s0_ref, s1_ref, 0)

    @pl.when(i + 1 < nb)
    def _():
        start(s0n_ref, s1n_ref, 1 - slot)

    _row_gather_wait(yb_hbm, ybuf.at[slot, 0], sems.at[slot, 0], rows)
    _row_gather_wait(yb_hbm, ybuf.at[slot, 1], sems.at[slot, 1], rows)
    route = route_ref[...]
    g0 = route[:, ROUTE_GATE0:ROUTE_GATE0 + 1]
    g1 = route[:, ROUTE_GATE1:ROUTE_GATE1 + 1]
    y = jnp.concatenate([g0 * ybuf[slot, 0, :, s, :] + g1 * ybuf[slot, 1, :, s, :] for s in range(ybuf.shape[3])],
                        axis=1)
    out_ref[...] = _rms(x2_ref[...] + y, gfin_ref[...])


def _moe_and_final(x2, u3, route, w_e1, w_e3, w_e2, g_final, tm, blk):
    n, d_model = x2.shape
    n_slab = d_model // LANES
    n_exp = w_e1.shape[0]
    a_total = n * EXPERT_TOPK
    n_blocks = -(-(a_total + n_exp * (blk - 1)) // blk)
    p_rows = n_blocks * blk

    fe = route[:, ROUTE_EXPERT0:ROUTE_EXPERT1 + 1].astype(I32).reshape(-1)
    onehot = (fe[:, None] == jnp.arange(n_exp, dtype=I32)[None, :]).astype(I32)
    csum = jnp.cumsum(onehot, axis=0)
    rank = jnp.sum((csum - onehot) * onehot, axis=1)
    counts = csum[-1]
    padded = (counts + blk - 1) // blk * blk
    pad_end = jnp.cumsum(padded)
    dest = (pad_end - padded)[fe] + rank
    ft = jnp.arange(a_total, dtype=I32) // EXPERT_TOPK
    slot_tok = jnp.zeros((p_rows,), I32).at[dest].set(ft)
    block_start = jnp.arange(n_blocks, dtype=I32) * blk
    block_exp = jnp.minimum(jnp.sum((pad_end[None, :] <= block_start[:, None]).astype(I32), axis=1), n_exp - 1)

    tok3 = slot_tok.reshape(n_blocks, 1, blk)
    smem_blk = lambda w, f: pl.BlockSpec((1, 1, w), f, memory_space=pltpu.SMEM)
    w_spec = lambda shp: pl.BlockSpec((1,) + shp[1:], lambda i, be: (be[i], 0, 0))
    yb = pl.pallas_call(
        _expert_kernel,
        grid_spec=pltpu.PrefetchScalarGridSpec(
            num_scalar_prefetch=1,
            grid=(n_blocks,),
            in_specs=[smem_blk(blk, lambda i, be: (i, 0, 0)),
                      smem_blk(blk, lambda i, be: (jnp.minimum(i + 1, n_blocks - 1), 0, 0)),
                      pl.BlockSpec(memory_space=pl.ANY),
                      w_spec(w_e1.shape), w_spec(w_e3.shape), w_spec(w_e2.shape)],
            out_specs=pl.BlockSpec((blk, n_slab, LANES), lambda i, be: (i, 0, 0)),
            scratch_shapes=[pltpu.VMEM((2, blk, n_slab, LANES), F32), pltpu.SemaphoreType.DMA((2,))],
        ),
        out_shape=jax.ShapeDtypeStruct((p_rows, n_slab, LANES), F32),
        compiler_params=_cparams(("arbitrary",)),
        name="moe_experts",
    )(block_exp, tok3, tok3, u3, w_e1.astype(BF16), w_e3.astype(BF16), w_e2.astype(BF16))

    tm = min(tm, n)
    assert n % tm == 0
    nt = n // tm
    d2 = dest.reshape(n, EXPERT_TOPK)
    s0 = d2[:, 0].reshape(nt, 1, tm)
    s1 = d2[:, 1].reshape(nt, 1, tm)
    cur = lambda i: (i, 0, 0)
    nxt = lambda i: (jnp.minimum(i + 1, nt - 1), 0, 0)
    smem_blk2 = lambda f: pl.BlockSpec((1, 1, tm), f, memory_space=pltpu.SMEM)
    row_spec = lambda w: pl.BlockSpec((tm, w), lambda i: (i, 0))
    return pl.pallas_call(
        _combine_kernel,
        grid=(nt,),
        in_specs=[smem_blk2(cur), smem_blk2(cur), smem_blk2(nxt), smem_blk2(nxt),
                  pl.BlockSpec(memory_space=pl.ANY), row_spec(d_model), row_spec(LANES), _const_spec((1, d_model))],
        out_specs=row_spec(d_model),
        out_shape=jax.ShapeDtypeStruct((n, d_model), F32),
        scratch_shapes=[pltpu.VMEM((2, EXPERT_TOPK, tm, n_slab, LANES), F32),
                        pltpu.SemaphoreType.DMA((2, EXPERT_TOPK))],
        compiler_params=_cparams(("arbitrary",)),
        name="moe_combine",
    )(s0, s1, s0, s1, yb, x2, route, g_final.reshape(1, d_model))


def _layer(x, pos, conv_buf, h0, attend, mk, mv, lw, g_final, tiles):
    batch, t_len, d_model = x.shape
    n = batch * t_len
    d_lru = lw['w_conv'].shape[1]
    d_att = N_HEADS * HEAD_DIM
    d_idx = IDX_HEADS * IDX_DIM
    x2d = x.reshape(n, d_model)
    xl, q, k, v, qi, kiwi, gl, ga = _in_proj(x2d, lw['g_mix'], lw['w_in_packed'], pos, t_len, d_lru, d_att, d_idx,
                                             tiles['in_proj'])
    att = attend(qi, q, kiwi, k, v)
    hs, h_last, new_buf = _rglru(xl.reshape(batch, t_len, d_lru), conv_buf, h0, lw['w_conv'], lw['b_conv'],
                                 lw['w_ra'], lw['b_ra'], lw['w_ri'], lw['b_ri'], lw['lru_lambda'],
                                 tiles['lru_t'], tiles['lru_b'])
    x1, qc = _mix(x2d, hs.reshape(n, d_lru), att, gl, ga, lw['w_lru_out'], lw['w_att_out'], lw['w_mix_out'],
                  lw['g_cross'], lw['w_cq'], tiles['mix'])
    x2, u3, route = _cross(x1, qc, mk, mv, lw['w_co'], lw['g_ffn'], lw['w_router'], lw['b_router'], batch, t_len,
                           tiles['cross'], tiles['cross_seqs'])
    y = _moe_and_final(x2, u3, route, lw['w_e1'], lw['w_e3'], lw['w_e2'], g_final, tiles['combine'],
                       tiles['moe_rows'])
    k5 = k.reshape(batch, t_len, N_HEADS, HEAD_DIM)
    v5 = v.reshape(batch, t_len, N_HEADS, HEAD_DIM)
    ki = kiwi[:, :IDX_DIM].reshape(batch, t_len, IDX_DIM)
    return y.reshape(batch, t_len, d_model), new_buf, h_last, k5, v5, ki


def kernel(x_prompt, mem_prompt, x_sample, cache_k, cache_v, cache_kidx, cache_mem_k, cache_mem_v, state_conv, state_lru, page_table, g_mix, w_in, w_conv, b_conv, w_ra, b_ra, w_ri, b_ri, lru_lambda, w_lru_out, w_att_out, w_mix_out, g_cross, g_mem, w_cq, w_mk, w_mv, w_co, g_ffn, w_rg, b_rg, w_re, b_re, w_e1, w_e3, w_e2, g_final):
    depth = w_in.shape[0]
    assert depth == 1, "the final norm is fused into the last layer's MoE combine; one layer supported"
    l = 0
    b_p, t_p, _ = x_prompt.shape
    b_s, t_s, _ = x_sample.shape
    past = page_table.shape[1] * cache_k.shape[2]
    d_lru = w_conv.shape[2]
    d_att = N_HEADS * HEAD_DIM
    d_idx = IDX_HEADS * IDX_DIM
    w_router, b_router = _pack_router(w_rg[l], b_rg[l], w_re[l], b_re[l])
    lw = {
        'g_mix': g_mix[l], 'w_in_packed': _pack_w_in(w_in[l], d_lru, d_att, d_idx),
        'w_conv': w_conv[l], 'b_conv': b_conv[l], 'w_ra': w_ra[l], 'b_ra': b_ra[l], 'w_ri': w_ri[l], 'b_ri': b_ri[l],
        'lru_lambda': lru_lambda[l], 'w_lru_out': w_lru_out[l], 'w_att_out': w_att_out[l], 'w_mix_out': w_mix_out[l],
        'g_cross': g_cross[l], 'w_cq': w_cq[l], 'w_co': w_co[l], 'g_ffn': g_ffn[l],
        'w_router': w_router, 'b_router': b_router, 'w_e1': w_e1[l], 'w_e3': w_e3[l], 'w_e2': w_e2[l],
    }

    mk_p, mv_p, mkb_p, mvb_p = _mem_kv(mem_prompt, g_mem[l], w_mk[l], w_mv[l])
    attend_p = functools.partial(_prompt_attention, batch=b_p, t_len=t_p)
    tiles_p = {'in_proj': 512, 'lru_t': 128, 'lru_b': 8, 'mix': 512, 'cross': 512, 'cross_seqs': 1,
               'combine': 256, 'moe_rows': 256}
    y_p, buf_p, h_p, k_p, v_p, ki_p = _layer(
        x_prompt, jnp.arange(t_p), jnp.zeros((b_p, CONV_W - 1, d_lru), F32), jnp.zeros((b_p, d_lru), F32),
        attend_p, mkb_p, mvb_p, lw, g_final, tiles_p)

    def attend_s(qi, q, kiwi, k, v):
        return _sample_attention(qi, q, kiwi, k, v, cache_k, cache_v, cache_kidx, page_table, l, t_s,
                                 pg_sel=16, pg_att=16)
    tiles_s = {'in_proj': 512, 'lru_t': 8, 'lru_b': 8, 'mix': 512, 'cross': t_s, 'cross_seqs': 8,
               'combine': 256, 'moe_rows': 128}
    flat_mem = lambda m: m.reshape(m.shape[0], m.shape[1], -1).astype(BF16)
    y_s, buf_s, h_s, k_s, v_s, ki_s = _layer(
        x_sample, past + jnp.arange(t_s), state_conv[l], state_lru[l], attend_s,
        flat_mem(cache_mem_k[l]), flat_mem(cache_mem_v[l]), lw, g_final, tiles_s)

    st = lambda a: a[None]
    return (y_p, y_s, st(k_p), st(v_p), st(ki_p), st(buf_p), st(h_p), st(mk_p), st(mv_p),
            st(k_s), st(v_s), st(ki_s), st(buf_s), st(h_s))
```

```python
import functools
import math

import jax
import jax.numpy as jnp
import numpy as np
from jax import lax
from jax.experimental import pallas as pl
from jax.experimental.pallas import tpu as pltpu

F32 = jnp.float32
BF16 = jnp.bfloat16
I32 = jnp.int32

N_HEADS = 8
HEAD_DIM = 64
IDX_HEADS = 8
IDX_DIM = 64
TOPK_MAX = 256
Q_BLOCK = 128
ROPE_THETA = 10000.0
CONV_W = 4
LRU_C = 8.0
LRU_BLOCKS = 8
C_HEADS = 4
N_GROUPS = 4
EXPERTS_PER_GROUP = 8
EXPERT_TOPK = 2
MOE_BLOCK = 128
NORM_EPS = 1e-6

LANES = 128
VMEM_LIMIT = 56 * 1024 * 1024


def _cparams(sem):
    return pltpu.CompilerParams(dimension_semantics=sem, vmem_limit_bytes=VMEM_LIMIT)


def _const_spec(shape):
    nd = len(shape)
    return pl.BlockSpec(shape, lambda *_: (0,) * nd)


def _rms(x, g):
    return x * lax.rsqrt(jnp.mean(x * x, axis=-1, keepdims=True) + NORM_EPS) * g


def _rope_rot(x, cos, sin_signed):
    w = x.shape[1]
    reps = w // LANES
    fwd = pltpu.roll(x, 32, axis=1)
    bwd = pltpu.roll(x, w - 32, axis=1)
    lane = lax.broadcasted_iota(I32, x.shape, 1)
    rot = jnp.where((lane % 64) < 32, bwd, fwd)
    if reps > 1:
        cos = jnp.concatenate([cos] * reps, axis=1)
        sin_signed = jnp.concatenate([sin_signed] * reps, axis=1)
    return x * cos + rot * sin_signed


def _in_proj_kernel(x_ref, g_ref, w_ref, cos_ref, sin_ref,
                    xl_ref, q_ref, k_ref, v_ref, qi_ref, kiwi_ref, gl_ref, ga_ref, *, d_lru, d_att, d_idx, d_model):
    u = _rms(x_ref[...], g_ref[...]).astype(BF16)
    cos = cos_ref[...]
    sin = sin_ref[...]

    def proj(c0, width):
        return jnp.dot(u, w_ref[:, c0:c0 + width], preferred_element_type=F32)

    c = 0
    xl_ref[...] = proj(c, d_lru); c += d_lru
    q_ref[...] = (_rope_rot(proj(c, d_att), cos, sin) * (HEAD_DIM ** -0.5)).astype(BF16); c += d_att
    k_ref[...] = _rope_rot(proj(c, d_att), cos, sin); c += d_att
    v_ref[...] = proj(c, d_att); c += d_att
    qi_ref[...] = (_rope_rot(proj(c, d_idx), cos, sin) * (IDX_DIM ** -0.5)).astype(BF16); c += d_idx
    kw = proj(c, LANES); c += LANES
    lane = lax.broadcasted_iota(I32, kw.shape, 1)
    kiwi_ref[...] = jnp.where(lane < IDX_DIM, _rope_rot(kw, cos, sin), kw * (IDX_HEADS ** -0.5))
    gl_ref[...] = jax.nn.sigmoid(proj(c, d_model)); c += d_model
    ga_ref[...] = jax.nn.sigmoid(proj(c, d_model))


def _rope_tables(pos, rows):
    inv = ROPE_THETA ** (-jnp.arange(0, HEAD_DIM, 2, dtype=F32) / HEAD_DIM)
    ang = pos.astype(F32)[:, None] * inv[None, :]
    cos = jnp.cos(ang)
    sin = jnp.sin(ang)
    cos_t = jnp.concatenate([cos, cos, cos, cos], axis=1)
    sin_t = jnp.concatenate([-sin, sin, -sin, sin], axis=1)
    reps = rows // pos.shape[0]
    if reps > 1:
        cos_t = jnp.tile(cos_t, (reps, 1))
        sin_t = jnp.tile(sin_t, (reps, 1))
    return cos_t, sin_t


def _in_proj(x2d, g, w_packed, pos, seq_len, d_lru, d_att, d_idx, tm):
    n, d_model = x2d.shape
    tm = min(tm, n)
    assert n % tm == 0
    rows = max(seq_len, tm)
    assert rows % tm == 0 and (tm % seq_len == 0 or seq_len % tm == 0)
    cos_t, sin_t = _rope_tables(pos, rows)
    nt = rows // tm
    row_spec = lambda w: pl.BlockSpec((tm, w), lambda i: (i, 0))
    tab_spec = pl.BlockSpec((tm, LANES), lambda i: (i % nt, 0))
    kern = functools.partial(_in_proj_kernel, d_lru=d_lru, d_att=d_att, d_idx=d_idx, d_model=d_model)
    out_shape = (
        jax.ShapeDtypeStruct((n, d_lru), F32),
        jax.ShapeDtypeStruct((n, d_att), BF16),
        jax.ShapeDtypeStruct((n, d_att), F32),
        jax.ShapeDtypeStruct((n, d_att), F32),
        jax.ShapeDtypeStruct((n, d_idx), BF16),
        jax.ShapeDtypeStruct((n, LANES), F32),
        jax.ShapeDtypeStruct((n, d_model), F32),
        jax.ShapeDtypeStruct((n, d_model), F32),
    )
    return pl.pallas_call(
        kern,
        grid=(n // tm,),
        in_specs=[row_spec(d_model), _const_spec((1, d_model)), _const_spec(w_packed.shape), tab_spec, tab_spec],
        out_specs=tuple(row_spec(s.shape[1]) for s in out_shape),
        out_shape=out_shape,
        compiler_params=_cparams(("parallel",)),
        name="in_proj",
    )(x2d, g.reshape(1, d_model), w_packed, cos_t, sin_t)


def _pack_w_in(w_in, d_lru, d_att, d_idx):
    d_model = w_in.shape[0]
    c = d_lru + 3 * d_att + d_idx
    kiwi = w_in[:, c:c + IDX_DIM + IDX_HEADS]
    pad = jnp.zeros((d_model, LANES - IDX_DIM - IDX_HEADS), w_in.dtype)
    return jnp.concatenate([w_in[:, :c], kiwi, pad, w_in[:, c + IDX_DIM + IDX_HEADS:]], axis=1).astype(BF16)


CONV_PAD = 8


def _softplus(x):
    return jnp.maximum(x, 0.0) + jnp.log1p(jnp.exp(-jnp.abs(x)))


def _rglru_kernel(xl_ref, buf0_ref, h0_ref, wconv_ref, bconv_ref, wra_ref, bra_ref, wri_ref, bri_ref, lam_ref,
                  hs_ref, hlast_ref, newbuf_ref, xpad_scr, a_scr, u_scr, h_scr, *, tc, bb):
    j = pl.program_id(1)
    d = xl_ref.shape[-1]
    hist = CONV_W - 1

    @pl.when(j == 0)
    def _():
        xpad_scr[:, 0:CONV_PAD - hist, :] = jnp.zeros((bb, CONV_PAD - hist, d), F32)
        xpad_scr[:, CONV_PAD - hist:CONV_PAD, :] = buf0_ref[...]
        h_scr[...] = h0_ref[...]

    xpad_scr[:, CONV_PAD:CONV_PAD + tc, :] = xl_ref[...]
    xc = jnp.zeros((bb, tc, d), F32) + bconv_ref[...]
    for i in range(CONV_W):
        off = CONV_PAD - hist + i
        xc = xc + xpad_scr[:, off:off + tc, :] * wconv_ref[i:i + 1, :]
    newbuf_ref[...] = xpad_scr[:, CONV_PAD + tc - hist:CONV_PAD + tc, :]
    xpad_scr[:, 0:CONV_PAD, :] = xpad_scr[:, tc:tc + CONV_PAD, :]

    xc2 = xc.reshape(bb * tc, d)
    xb = xc2.astype(BF16)
    r = jax.nn.sigmoid(jnp.dot(xb, wra_ref[...], preferred_element_type=F32) + bra_ref[...])
    g = jax.nn.sigmoid(jnp.dot(xb, wri_ref[...], preferred_element_type=F32) + bri_ref[...])
    log_a = (-LRU_C) * r * _softplus(-lam_ref[...])
    a = jnp.exp(log_a)
    mult = jnp.sqrt(-jnp.tanh(log_a) * (a * a + 1.0))
    a_scr[...] = a.reshape(bb, tc, d)
    u_scr[...] = (mult * (g * xc2)).reshape(bb, tc, d)

    def step(t, hs):
        new = []
        for b in range(bb):
            h = a_scr[b, pl.ds(t, 1), :] * hs[b] + u_scr[b, pl.ds(t, 1), :]
            u_scr[b, pl.ds(t, 1), :] = h
            new.append(h)
        return tuple(new)

    h_fin = lax.fori_loop(0, tc, step, tuple(h_scr[b] for b in range(bb)))
    for b in range(bb):
        h_scr[b] = h_fin[b]
        hlast_ref[b] = h_fin[b]
    hs_ref[...] = u_scr[...].astype(hs_ref.dtype)


def _block_diag(w):
    nb, c, _ = w.shape
    eye = jnp.eye(nb, dtype=w.dtype)
    return (eye[:, None, :, None] * w[:, :, None, :]).reshape(nb * c, nb * c)


def _rglru(xl, buf0, h0, w_conv, b_conv, w_ra, b_ra, w_ri, b_ri, lam, tc, bb):
    b, t, d = xl.shape
    tc = min(tc, t)
    bb = min(bb, b)
    assert t % tc == 0 and b % bb == 0 and tc % 8 == 0
    row = lambda a: a.reshape(1, d)
    kern = functools.partial(_rglru_kernel, tc=tc, bb=bb)
    hist = CONV_W - 1
    out_shape = (
        jax.ShapeDtypeStruct((b, t, d), BF16 if tc % 16 == 0 else F32),
        jax.ShapeDtypeStruct((b, 1, d), F32),
        jax.ShapeDtypeStruct((b, hist, d), F32),
    )
    hs, h_last, new_buf = pl.pallas_call(
        kern,
        grid=(b // bb, t // tc),
        in_specs=[
            pl.BlockSpec((bb, tc, d), lambda i, j: (i, j, 0)),
            pl.BlockSpec((bb, hist, d), lambda i, j: (i, 0, 0)),
            pl.BlockSpec((bb, 1, d), lambda i, j: (i, 0, 0)),
            _const_spec((CONV_W, d)), _const_spec((1, d)),
            _const_spec((d, d)), _const_spec((1, d)), _const_spec((d, d)), _const_spec((1, d)), _const_spec((1, d)),
        ],
        out_specs=(
            pl.BlockSpec((bb, tc, d), lambda i, j: (i, j, 0)),
            pl.BlockSpec((bb, 1, d), lambda i, j: (i, 0, 0)),
            pl.BlockSpec((bb, hist, d), lambda i, j: (i, 0, 0)),
        ),
        out_shape=out_shape,
        scratch_shapes=[
            pltpu.VMEM((bb, tc + CONV_PAD, d), F32),
            pltpu.VMEM((bb, tc, d), F32),
            pltpu.VMEM((bb, tc, d), F32),
            pltpu.VMEM((bb, 1, d), F32),
        ],
        compiler_params=_cparams(("parallel", "arbitrary")),
        name="rglru",
    )(xl, buf0, h0.reshape(b, 1, d), w_conv, row(b_conv),
      _block_diag(w_ra).astype(BF16), row(b_ra), _block_diag(w_ri).astype(BF16), row(b_ri), row(lam))
    return hs, h_last.reshape(b, d), new_buf


INT_MIN = -2 ** 31
NEG_INF_KEY = -2139095041
KEY_GROUP = 4
SELECT_SEQS = 8


def _sortable_key(score):
    score = jnp.where(score == 0.0, 0.0, score)
    bits = pltpu.bitcast(score, I32)
    return bits ^ ((bits >> 31) & 0x7FFFFFFF)


def _pair_block_diag(x):
    lane = lax.broadcasted_iota(I32, x.shape, 1)
    zero = jnp.zeros_like(x)
    return jnp.concatenate([jnp.where(lane < 64, x, zero), jnp.where(lane >= 64, x, zero)], axis=0)


def _dot_nt(a, b):
    return lax.dot_general(a, b, (((1,), (1,)), ((), ())), preferred_element_type=F32)


def _selection_bias(kk, eq, tie_rank, theta, need):
    take = jnp.where(eq, tie_rank, jnp.inf) <= need
    b = jnp.where(take, 0.0, -jnp.inf)
    b = jnp.where(kk > theta, 0.0, b)
    return jnp.where(kk == NEG_INF_KEY, -jnp.inf, b)


def _prompt_attn_kernel(qi_ref, q_ref, kiwi_ref, k_ref, v_ref, att_ref,
                        kdup_scr, kbf_scr, vT_scr, qibd_scr, qbd_scr, key_scr, bias_scr, logit_scr, oT_scr, acc_scr,
                        *, topk, qb):
    j = pl.program_id(1)
    t_len = k_ref.shape[0]
    n_pairs = q_ref.shape[1] // LANES
    nk = j + 1

    def rows_of(c):
        return pl.ds(pl.multiple_of(c * qb, qb), qb)

    @pl.when(j == 0)
    def _():
        def prep(c, carry):
            rows = rows_of(c)
            kw = kiwi_ref[rows, :]
            lane = lax.broadcasted_iota(I32, kw.shape, 1)
            kdup_scr[rows, :] = jnp.where(lane < IDX_DIM, kw, pltpu.roll(kw, IDX_DIM, axis=1)).astype(BF16)
            kbf_scr[rows, :] = k_ref[rows, :].astype(BF16)
            vT_scr[:, rows] = v_ref[rows, :].T.astype(BF16)
            return carry
        lax.fori_loop(0, t_len // qb, prep, 0)

    for p in range(n_pairs):
        qibd_scr[p] = _pair_block_diag(qi_ref[:, p * LANES:(p + 1) * LANES])
        qbd_scr[p] = _pair_block_diag(q_ref[:, p * LANES:(p + 1) * LANES])
    w_t = kiwi_ref[rows_of(j), :].T

    def key_loop(body, init):
        wide = KEY_GROUP * qb
        n_groups = (nk + KEY_GROUP - 1) // KEY_GROUP
        return lax.fori_loop(0, n_groups, lambda i, c: body(pl.multiple_of(i * wide, wide), wide, c), init)

    def idx_body(r0, nr, carry):
        rows = pl.ds(r0, nr)
        kd = kdup_scr[rows, :]
        sc = jnp.zeros((nr, qb), F32)
        for p in range(n_pairs):
            s2 = jnp.maximum(_dot_nt(kd, qibd_scr[p]), 0.0)
            h = IDX_DIM + 2 * p
            sc = sc + w_t[h:h + 1, :] * s2[:, :qb] + w_t[h + 1:h + 2, :] * s2[:, qb:]
        kpos = r0 + lax.broadcasted_iota(I32, (nr, qb), 0)
        qpos = j * qb + lax.broadcasted_iota(I32, (nr, qb), 1)
        key_scr[rows, :] = _sortable_key(jnp.where(kpos <= qpos, sc, -jnp.inf))
        return carry
    key_loop(idx_body, 0)

    def count(pred):
        def body(r0, nr, acc):
            hit = jnp.where(pred(key_scr[pl.ds(r0, nr), :]), 1, 0).astype(I32)
            return acc + jnp.sum(hit.reshape(nr // 8, 8, qb), axis=0)
        return jnp.sum(key_loop(body, jnp.zeros((8, qb), I32)), axis=0, keepdims=True)

    def bit_step(i, prefix):
        cand_u = prefix | jnp.left_shift(jnp.int32(1), 31 - i)
        cand_s = cand_u ^ INT_MIN
        return jnp.where(count(lambda kk: kk >= cand_s) >= topk, cand_u, prefix)
    theta = lax.fori_loop(0, 32, bit_step, jnp.zeros((1, qb), I32)) ^ INT_MIN

    need = (topk - count(lambda kk: kk > theta)).astype(F32)
    tri = jnp.where(lax.broadcasted_iota(I32, (qb, qb), 0) >= lax.broadcasted_iota(I32, (qb, qb), 1),
                    1.0, 0.0).astype(BF16)

    def mask_body(r0, nr, run):
        for s in range(nr // qb):
            rows = pl.ds(r0 + s * qb, qb)
            kk = key_scr[rows, :]
            eq = kk == theta
            pre = jnp.dot(tri, jnp.where(eq, 1.0, 0.0).astype(BF16), preferred_element_type=F32)
            bias_scr[rows, :] = _selection_bias(kk, eq, run + pre, theta, need)
            run = run + pre[qb - 1:qb, :]
        return run
    key_loop(mask_body, jnp.zeros((1, qb), F32))

    def pair_lanes(p):
        return slice(p * LANES, (p + 1) * LANES)

    def pass1(r0, nr, m8s):
        rows = pl.ds(r0, nr)
        bias = bias_scr[rows, :]
        bias2 = jnp.concatenate([bias, bias], axis=1)
        new = []
        for p in range(n_pairs):
            lg = _dot_nt(kbf_scr[rows, pair_lanes(p)], qbd_scr[p]) + bias2
            logit_scr[p, rows, :] = lg
            new.append(jnp.maximum(m8s[p], jnp.max(lg.reshape(nr // 8, 8, 2 * qb), axis=0)))
        return tuple(new)
    m8s = key_loop(pass1, tuple(jnp.full((8, 2 * qb), -jnp.inf, F32) for _ in range(n_pairs)))
    ms = [jnp.max(m8, axis=0, keepdims=True) for m8 in m8s]

    acc_scr[...] = jnp.zeros(acc_scr.shape, F32)

    def pass2(r0, nr, l8s):
        rows = pl.ds(r0, nr)
        new = []
        for p in range(n_pairs):
            pe = jnp.exp(logit_scr[p, rows, :] - ms[p])
            new.append(l8s[p] + jnp.sum(pe.reshape(nr // 8, 8, 2 * qb), axis=0))
            acc_scr[p] += jnp.dot(vT_scr[pair_lanes(p), rows], pe.astype(BF16), preferred_element_type=F32)
        return tuple(new)
    l8s = key_loop(pass2, tuple(jnp.zeros((8, 2 * qb), F32) for _ in range(n_pairs)))

    for p in range(n_pairs):
        o = acc_scr[p] / jnp.sum(l8s[p], axis=0, keepdims=True)
        oT_scr[p * LANES:p * LANES + HEAD_DIM, :] = o[:HEAD_DIM, :qb]
        oT_scr[p * LANES + HEAD_DIM:(p + 1) * LANES, :] = o[HEAD_DIM:, qb:]
    att_ref[...] = oT_scr[...].T.astype(att_ref.dtype)


def _prompt_attention(qi, q, kiwi, k, v, batch, t_len):
    n, d_att = q.shape
    qb = min(Q_BLOCK, t_len)
    assert qb == LANES and t_len % (KEY_GROUP * qb) == 0
    nqb = t_len // qb
    topk = min(TOPK_MAX, t_len // 4)
    n_pairs = d_att // LANES
    kern = functools.partial(_prompt_attn_kernel, topk=topk, qb=qb)
    blk_spec = lambda w: pl.BlockSpec((qb, w), lambda b, j: (b * nqb + j, 0))
    seq_spec = lambda w: pl.BlockSpec((t_len, w), lambda b, j: (b, 0))
    return pl.pallas_call(
        kern,
        grid=(batch, nqb),
        in_specs=[blk_spec(qi.shape[1]), blk_spec(d_att), seq_spec(LANES), seq_spec(d_att), seq_spec(d_att)],
        out_specs=blk_spec(d_att),
        out_shape=jax.ShapeDtypeStruct((n, d_att), BF16),
        scratch_shapes=[
            pltpu.VMEM((t_len, LANES), BF16),
            pltpu.VMEM((t_len, d_att), BF16),
            pltpu.VMEM((d_att, t_len), BF16),
            pltpu.VMEM((n_pairs, 2 * qb, LANES), BF16),
            pltpu.VMEM((n_pairs, 2 * qb, LANES), BF16),
            pltpu.VMEM((t_len, qb), I32),
            pltpu.VMEM((t_len, qb), F32),
            pltpu.VMEM((n_pairs, t_len, 2 * qb), F32),
            pltpu.VMEM((d_att, qb), F32),
            pltpu.VMEM((n_pairs, LANES, 2 * qb), F32),
        ],
        compiler_params=_cparams(("parallel", "arbitrary")),
        name="prompt_attn",
    )(qi, q, kiwi, k, v)


def _sample_score_kernel(pt_ref, qi_ref, wi_ref, kinew_ref, *rest, pg, past, page, t_new):
    del pt_ref
    page_refs = rest[:pg]
    key_ref = rest[pg]
    g = pl.program_id(1)
    qi = qi_ref[...]
    wi = wi_ref[...]

    def head_sum(s):
        s = jnp.maximum(s, 0.0) * wi
        return jnp.sum(s.reshape(IDX_HEADS, t_new, s.shape[1]), axis=0)

    k_t = jnp.concatenate([r[0, 0] for r in page_refs], axis=1).astype(BF16)
    off = pl.multiple_of(g * (pg * page), page)
    key_ref[0, :, pl.ds(off, pg * page)] = _sortable_key(head_sum(jnp.dot(qi, k_t, preferred_element_type=F32)))

    @pl.when(g == pl.num_programs(1) - 1)
    def _():
        knew = jnp.concatenate([kinew_ref[:, :IDX_DIM], jnp.zeros((page - t_new, IDX_DIM), F32)], axis=0)
        sc = head_sum(_dot_nt(qi, knew.astype(BF16)))
        tok = lax.broadcasted_iota(I32, sc.shape, 0)
        kk_i = lax.broadcasted_iota(I32, sc.shape, 1)
        key_ref[0, :, past:past + page] = _sortable_key(jnp.where(kk_i <= tok, sc, -jnp.inf))


def _sample_threshold_kernel(key_ref, bias_ref, *, topk, page):
    n_seq, t_new, l_pad = key_ref.shape
    rows = n_seq * t_new
    n_chunks = l_pad // page

    def chunk(c):
        return key_ref[:, :, c * page:(c + 1) * page].reshape(rows, page)

    def count(pred):
        acc = jnp.zeros((rows, page), I32)
        for c in range(n_chunks):
            acc = acc + jnp.where(pred(chunk(c)), 1, 0).astype(I32)
        return jnp.sum(acc, axis=1, keepdims=True)

    def bit_step(i, prefix):
        cand_u = prefix | jnp.left_shift(jnp.int32(1), 31 - i)
        cand_s = cand_u ^ INT_MIN
        return jnp.where(count(lambda kk: kk >= cand_s) >= topk, cand_u, prefix)
    theta = lax.fori_loop(0, 32, bit_step, jnp.zeros((rows, 1), I32)) ^ INT_MIN
    need = (topk - count(lambda kk: kk > theta)).astype(F32)

    row_i = lax.broadcasted_iota(I32, (page, page), 0)
    col_i = lax.broadcasted_iota(I32, (page, page), 1)
    triu = jnp.where(row_i <= col_i, 1.0, 0.0).astype(BF16)
    run = jnp.zeros((rows, 1), F32)
    for c in range(n_chunks):
        kk = chunk(c)
        eq = kk == theta
        pre = jnp.dot(jnp.where(eq, 1.0, 0.0).astype(BF16), triu, preferred_element_type=F32)
        bias = _selection_bias(kk, eq, run + pre, theta, need)
        bias_ref[:, :, c * page:(c + 1) * page] = bias.reshape(n_seq, t_new, page)
        run = run + pre[:, page - 1:page]


def _sample_attn_kernel(pt_ref, q_ref, bias_ref, knew_ref, vnew_ref, *rest, pg, past, page, t_new):
    del pt_ref
    k_refs = rest[:pg]
    v_refs = rest[pg:2 * pg]
    out_ref = rest[2 * pg]
    m_scr, l_scr, acc_scr = rest[2 * pg + 1:]
    g = pl.program_id(1)
    qbd = q_ref[...]

    @pl.when(g == 0)
    def _():
        m_scr[...] = jnp.full(m_scr.shape, -jnp.inf, F32)
        l_scr[...] = jnp.zeros(l_scr.shape, F32)
        acc_scr[...] = jnp.zeros(acc_scr.shape, F32)

    def attend(lg, bias, pv_fn):
        lg = lg + jnp.concatenate([bias] * N_HEADS, axis=0)
        m_old = m_scr[...]
        m_new = jnp.maximum(m_old, jnp.max(lg, axis=1, keepdims=True))
        m_safe = jnp.where(m_new == -jnp.inf, 0.0, m_new)
        alpha = jnp.exp(m_old - m_safe)
        pe = jnp.exp(lg - m_safe)
        l_scr[...] = alpha * l_scr[...] + jnp.sum(pe, axis=1, keepdims=True)
        acc_scr[...] = alpha * acc_scr[...] + pv_fn(pe.astype(BF16))
        m_scr[...] = m_new

    k_t = jnp.concatenate([r[0, 0] for r in k_refs], axis=1).astype(BF16)
    v_t = jnp.concatenate([r[0, 0] for r in v_refs], axis=1).astype(BF16)
    off = pl.multiple_of(g * (pg * page), page)
    attend(jnp.dot(qbd, k_t, preferred_element_type=F32), bias_ref[0, :, pl.ds(off, pg * page)],
           lambda pb: _dot_nt(pb, v_t))

    @pl.when(g == pl.num_programs(1) - 1)
    def _():
        pad = jnp.zeros((page - t_new, knew_ref.shape[1]), F32)
        k_new = jnp.concatenate([knew_ref[...], pad], axis=0).astype(BF16)
        v_new = jnp.concatenate([vnew_ref[...], pad], axis=0).astype(BF16)
        attend(_dot_nt(qbd, k_new), bias_ref[0, :, past:past + page],
               lambda pb: jnp.dot(pb, v_new, preferred_element_type=F32))
        out_ref[0] = acc_scr[...] / l_scr[...]


def _to_head_major(x, batch, t_new, heads, dim):
    return x.reshape(batch, t_new, heads, dim).transpose(0, 2, 1, 3).reshape(batch * heads * t_new, dim)


def _sample_attention(qi, q, kiwi, k_new, v_new, cache_k, cache_v, cache_kidx, page_table, layer, t_new, pg_sel, pg_att):
    batch, n_pages = page_table.shape
    depth, n_pool, page = cache_k.shape[:3]
    past = n_pages * page
    pg_sel = min(pg_sel, n_pages)
    pg_att = min(pg_att, n_pages)
    assert n_pages % pg_sel == 0 and n_pages % pg_att == 0 and page == LANES and t_new == 8
    l_pad = past + page
    topk = min(TOPK_MAX, (past + t_new) // 4)
    rows = IDX_HEADS * t_new
    d_att = N_HEADS * HEAD_DIM

    kidx_t = cache_kidx.transpose(0, 1, 3, 2)
    k_t = cache_k.transpose(0, 1, 3, 4, 2).reshape(depth, n_pool, d_att, page)
    v_t = cache_v.transpose(0, 1, 3, 4, 2).reshape(depth, n_pool, d_att, page)

    qi_hm = _to_head_major(qi, batch, t_new, IDX_HEADS, IDX_DIM)
    wi_col = kiwi[:, IDX_DIM:IDX_DIM + IDX_HEADS].reshape(batch, t_new, IDX_HEADS).transpose(0, 2, 1).reshape(-1, 1)
    seq_spec = lambda r, w: pl.BlockSpec((r, w), lambda b, g, pt: (b, 0))
    bias_spec = pl.BlockSpec((1, t_new, l_pad), lambda b, g, pt: (b, 0, 0))

    def page_spec(arr, i, pg):
        return pl.BlockSpec((1, 1) + arr.shape[2:], lambda b, g, pt: (layer, pt[b, g * pg + i], 0, 0))

    keys = pl.pallas_call(
        functools.partial(_sample_score_kernel, pg=pg_sel, past=past, page=page, t_new=t_new),
        grid_spec=pltpu.PrefetchScalarGridSpec(
            num_scalar_prefetch=1,
            grid=(batch, n_pages // pg_sel),
            in_specs=[seq_spec(rows, IDX_DIM), seq_spec(rows, 1), seq_spec(t_new, LANES)]
                     + [page_spec(kidx_t, i, pg_sel) for i in range(pg_sel)],
            out_specs=bias_spec,
        ),
        out_shape=jax.ShapeDtypeStruct((batch, t_new, l_pad), I32),
        compiler_params=_cparams(("parallel", "arbitrary")),
        name="sample_score",
    )(page_table, qi_hm, wi_col, kiwi, *([kidx_t] * pg_sel))

    sel_seqs = min(SELECT_SEQS, batch)
    assert batch % sel_seqs == 0
    sel_spec = pl.BlockSpec((sel_seqs, t_new, l_pad), lambda b: (b, 0, 0))
    bias = pl.pallas_call(
        functools.partial(_sample_threshold_kernel, topk=topk, page=page),
        grid=(batch // sel_seqs,),
        in_specs=[sel_spec],
        out_specs=sel_spec,
        out_shape=jax.ShapeDtypeStruct((batch, t_new, l_pad), F32),
        compiler_params=_cparams(("parallel",)),
        name="sample_select",
    )(keys)

    q4 = q.reshape(batch, t_new, N_HEADS, HEAD_DIM)
    eye = jnp.eye(N_HEADS, dtype=q.dtype)
    qbd = (q4.transpose(0, 2, 1, 3)[:, :, :, None, :] * eye[None, :, None, :, None]).reshape(batch * rows, d_att)
    out = pl.pallas_call(
        functools.partial(_sample_attn_kernel, pg=pg_att, past=past, page=page, t_new=t_new),
        grid_spec=pltpu.PrefetchScalarGridSpec(
            num_scalar_prefetch=1,
            grid=(batch, n_pages // pg_att),
            in_specs=[seq_spec(rows, d_att), bias_spec, seq_spec(t_new, d_att), seq_spec(t_new, d_att)]
                     + [page_spec(k_t, i, pg_att) for i in range(pg_att)]
                     + [page_spec(v_t, i, pg_att) for i in range(pg_att)],
            out_specs=pl.BlockSpec((1, rows, d_att), lambda b, g, pt: (b, 0, 0)),
            scratch_shapes=[pltpu.VMEM((rows, 1), F32), pltpu.VMEM((rows, 1), F32), pltpu.VMEM((rows, d_att), F32)],
        ),
        out_shape=jax.ShapeDtypeStruct((batch, rows, d_att), F32),
        compiler_params=_cparams(("parallel", "arbitrary")),
        name="sample_attn",
    )(page_table, qbd, bias, k_new, v_new, *([k_t] * pg_att), *([v_t] * pg_att))
    o5 = out.reshape(batch, N_HEADS, t_new, N_HEADS, HEAD_DIM)
    att = jnp.stack([o5[:, h, :, h, :] for h in range(N_HEADS)], axis=2)
    return att.reshape(batch * t_new, d_att).astype(BF16)


def _mix_kernel(x_ref, hs_ref, att_ref, gl_ref, ga_ref, wl_ref, wa_ref, wm_ref, gc_ref, wcq_ref, x1_ref, qc_ref):
    lru = jnp.dot(hs_ref[...].astype(BF16), wl_ref[...], preferred_element_type=F32)
    att = jnp.dot(att_ref[...], wa_ref[...], preferred_element_type=F32)
    mixed = gl_ref[...] * lru + ga_ref[...] * att
    x1 = x_ref[...] + jnp.dot(mixed.astype(BF16), wm_ref[...], preferred_element_type=F32)
    x1_ref[...] = x1
    qc_ref[...] = jnp.dot(_rms(x1, gc_ref[...]).astype(BF16), wcq_ref[...], preferred_element_type=F32)


def _mix(x2d, hs, att, gl, ga, w_lru_out, w_att_out, w_mix_out, g_cross, w_cq, tm):
    n, d_model = x2d.shape
    tm = min(tm, n)
    assert n % tm == 0
    d_cross = w_cq.shape[1]
    row_spec = lambda w: pl.BlockSpec((tm, w), lambda i: (i, 0))
    return pl.pallas_call(
        _mix_kernel,
        grid=(n // tm,),
        in_specs=[row_spec(d_model), row_spec(hs.shape[1]), row_spec(att.shape[1]), row_spec(d_model), row_spec(d_model),
                  _const_spec(w_lru_out.shape), _const_spec(w_att_out.shape), _const_spec(w_mix_out.shape),
                  _const_spec((1, d_model)), _const_spec(w_cq.shape)],
        out_specs=(row_spec(d_model), row_spec(d_cross)),
        out_shape=(jax.ShapeDtypeStruct((n, d_model), F32), jax.ShapeDtypeStruct((n, d_cross), F32)),
        compiler_params=_cparams(("parallel",)),
        name="mix_out",
    )(x2d, hs, att, gl, ga, w_lru_out.astype(BF16), w_att_out.astype(BF16), w_mix_out.astype(BF16),
      g_cross.reshape(1, d_model), w_cq.astype(BF16))


def _mem_kv_kernel(mem_ref, g_ref, wk_ref, wv_ref, mk_ref, mv_ref, mkb_ref, mvb_ref):
    m = _rms(mem_ref[0], g_ref[...]).astype(BF16)
    mk = jnp.dot(m, wk_ref[...], preferred_element_type=F32)
    mv = jnp.dot(m, wv_ref[...], preferred_element_type=F32)
    hd = mk_ref.shape[-1]
    for h in range(C_HEADS):
        mk_ref[0, :, h, :] = mk[:, h * hd:(h + 1) * hd]
        mv_ref[0, :, h, :] = mv[:, h * hd:(h + 1) * hd]
    mkb_ref[0] = mk.astype(BF16)
    mvb_ref[0] = mv.astype(BF16)


def _mem_kv(mem, g_mem, w_mk, w_mv):
    b, s, d_model = mem.shape
    d_cross = w_mk.shape[1]
    hd = d_cross // C_HEADS
    kv_spec = pl.BlockSpec((1, s, C_HEADS, hd), lambda i: (i, 0, 0, 0))
    kv_shape = jax.ShapeDtypeStruct((b, s, C_HEADS, hd), F32)
    flat_spec = pl.BlockSpec((1, s, d_cross), lambda i: (i, 0, 0))
    flat_shape = jax.ShapeDtypeStruct((b, s, d_cross), BF16)
    return pl.pallas_call(
        _mem_kv_kernel,
        grid=(b,),
        in_specs=[pl.BlockSpec((1, s, d_model), lambda i: (i, 0, 0)), _const_spec((1, d_model)),
                  _const_spec(w_mk.shape), _const_spec(w_mv.shape)],
        out_specs=(kv_spec, kv_spec, flat_spec, flat_spec),
        out_shape=(kv_shape, kv_shape, flat_shape, flat_shape),
        compiler_params=_cparams(("parallel",)),
        name="mem_kv",
    )(mem, g_mem.reshape(1, d_model), w_mk.astype(BF16), w_mv.astype(BF16))


ROUTE_EXPERT0, ROUTE_EXPERT1, ROUTE_GATE0, ROUTE_GATE1 = 0, 1, 2, 3


def _lane_min_where(cond, lane):
    return jnp.min(jnp.where(cond, lane, LANES), axis=1, keepdims=True)


def _cross_kernel(x1_ref, qc_ref, mk_ref, mv_ref, wco_ref, gf_ref, wr_ref, br_ref, x2_ref, u3_ref, route_ref):
    n_seq = mk_ref.shape[0]
    tq = x1_ref.shape[0] // n_seq
    hd = mk_ref.shape[-1] // C_HEADS
    scale = hd ** -0.5
    seq_outs = []
    for sq in range(n_seq):
        rows = slice(sq * tq, (sq + 1) * tq)
        outs = []
        for h in range(C_HEADS):
            cols = slice(h * hd, (h + 1) * hd)
            s = _dot_nt(qc_ref[rows, cols].astype(BF16), mk_ref[sq, :, cols]) * scale
            pe = jnp.exp(s - jnp.max(s, axis=1, keepdims=True))
            o = jnp.dot(pe.astype(BF16), mv_ref[sq, :, cols], preferred_element_type=F32)
            outs.append(o / jnp.sum(pe, axis=1, keepdims=True))
        seq_outs.append(jnp.concatenate(outs, axis=1))
    o = jnp.concatenate(seq_outs, axis=0).astype(BF16)
    x2 = x1_ref[...] + jnp.dot(o, wco_ref[...], preferred_element_type=F32)
    x2_ref[...] = x2
    u3 = _rms(x2, gf_ref[...])
    for s in range(u3_ref.shape[1]):
        u3_ref[:, s, :] = u3[:, s * LANES:(s + 1) * LANES]

    logits = jnp.dot(u3.astype(BF16), wr_ref[...], preferred_element_type=F32) + br_ref[...]
    lane = lax.broadcasted_iota(I32, logits.shape, 1)
    lg = jnp.where(lane < N_GROUPS, logits, -jnp.inf)
    g_max = jnp.max(lg, axis=1, keepdims=True)
    g_sel = _lane_min_where(lg == g_max, lane)
    p_grp = 1.0 / jnp.sum(jnp.exp(lg - g_max), axis=1, keepdims=True)
    e_lo = N_GROUPS + g_sel * EXPERTS_PER_GROUP
    le = jnp.where((lane >= e_lo) & (lane < e_lo + EXPERTS_PER_GROUP), logits, -jnp.inf)
    m1 = jnp.max(le, axis=1, keepdims=True)
    i1 = _lane_min_where(le == m1, lane)
    le2 = jnp.where(lane == i1, -jnp.inf, le)
    m2 = jnp.max(le2, axis=1, keepdims=True)
    i2 = _lane_min_where(le2 == m2, lane)
    e2 = jnp.exp(m2 - m1)
    inv = p_grp / (1.0 + e2)
    route = jnp.where(lane == ROUTE_EXPERT0, (i1 - N_GROUPS).astype(F32), 0.0)
    route = jnp.where(lane == ROUTE_EXPERT1, (i2 - N_GROUPS).astype(F32), route)
    route = jnp.where(lane == ROUTE_GATE0, inv, route)
    route_ref[...] = jnp.where(lane == ROUTE_GATE1, inv * e2, route)


def _pack_router(w_rg, b_rg, w_re, b_re):
    d_model = w_rg.shape[0]
    used = w_rg.shape[1] + w_re.shape[1]
    w = jnp.concatenate([w_rg, w_re, jnp.zeros((d_model, LANES - used), w_rg.dtype)], axis=1).astype(BF16)
    b = jnp.concatenate([b_rg, b_re, jnp.zeros((LANES - used,), b_rg.dtype)]).reshape(1, LANES)
    return w, b


def _cross(x1, qc, mk, mv, w_co, g_ffn, w_router, b_router, batch, t_len, tq, n_seq):
    n, d_model = x1.shape
    tq = min(tq, t_len)
    n_seq = min(n_seq, batch)
    assert t_len % tq == 0 and batch % n_seq == 0 and (n_seq == 1 or tq == t_len)
    nq = t_len // tq
    d_cross = qc.shape[1]
    rows = n_seq * tq
    row_spec = lambda w: pl.BlockSpec((rows, w), lambda b, j: (b * nq + j, 0))
    mem_spec = pl.BlockSpec((n_seq,) + mk.shape[1:], lambda b, j: (b, 0, 0))
    return pl.pallas_call(
        _cross_kernel,
        grid=(batch // n_seq, nq),
        in_specs=[row_spec(d_model), row_spec(d_cross), mem_spec, mem_spec, _const_spec(w_co.shape),
                  _const_spec((1, d_model)), _const_spec(w_router.shape), _const_spec((1, LANES))],
        out_specs=(row_spec(d_model), pl.BlockSpec((rows, d_model // LANES, LANES), lambda b, j: (b * nq + j, 0, 0)),
                   row_spec(LANES)),
        out_shape=(jax.ShapeDtypeStruct((n, d_model), F32), jax.ShapeDtypeStruct((n, d_model // LANES, LANES), F32),
                   jax.ShapeDtypeStruct((n, LANES), F32)),
        compiler_params=_cparams(("parallel", "parallel")),
        name="cross_router",
    )(x1, qc, mk, mv, w_co.astype(BF16), g_ffn.reshape(1, d_model), w_router, b_router)


DMA_PRIORITIES = 2


def _row_gather_start(src_hbm, idx_ref, dst, sem, n_rows):
    def issue(i, carry):
        for u in range(DMA_PRIORITIES):
            r = i * DMA_PRIORITIES + u
            pltpu.make_async_copy(src_hbm.at[pl.ds(idx_ref[0, 0, r], 1)], dst.at[pl.ds(r, 1)], sem).start(priority=u)
        return carry
    lax.fori_loop(0, n_rows // DMA_PRIORITIES, issue, 0)


def _row_gather_wait(src_hbm, dst, sem, n_rows):
    pltpu.make_async_copy(src_hbm.at[pl.ds(0, n_rows)], dst, sem).wait()


EXPERT_CHUNKS = 4


def _expert_kernel(be_ref, tok_ref, tok_next_ref, x_hbm, w1_ref, w3_ref, w2_ref, y_ref, xbuf, yacc, sems):
    del be_ref
    i = pl.program_id(0)
    nb = pl.num_programs(0)
    rows = xbuf.shape[1]
    slot = i % 2
    nxt = 1 - slot

    @pl.when(i == 0)
    def _():
        _row_gather_start(x_hbm, tok_ref, xbuf.at[0], sems.at[0], rows)

    _row_gather_wait(x_hbm, xbuf.at[slot], sems.at[slot], rows)
    n_slab = xbuf.shape[2]
    x = jnp.concatenate([xbuf[slot, :, s, :] for s in range(n_slab)], axis=1).astype(BF16)

    d_hidden = w1_ref.shape[2]
    cw = d_hidden // EXPERT_CHUNKS
    rpc = rows // EXPERT_CHUNKS
    for c in range(EXPERT_CHUNKS):
        for r in range(c * rpc, (c + 1) * rpc):
            pltpu.make_async_copy(x_hbm.at[pl.ds(tok_next_ref[0, 0, r], 1)], xbuf.at[nxt, pl.ds(r, 1)],
                                  sems.at[nxt]).start(priority=r % DMA_PRIORITIES)
        cols = slice(c * cw, (c + 1) * cw)
        a = jnp.dot(x, w1_ref[0, :, cols], preferred_element_type=F32)
        b = jnp.dot(x, w3_ref[0, :, cols], preferred_element_type=F32)
        h = (a * jax.nn.sigmoid(a) * b).astype(BF16)
        part = jnp.dot(h, w2_ref[0, cols, :], preferred_element_type=F32)
        if c == 0:
            yacc[...] = part
        else:
            yacc[...] += part
    for s in range(n_slab):
        y_ref[:, s, :] = yacc[:, s * LANES:(s + 1) * LANES]

    @pl.when(i == nb - 1)
    def _():
        _row_gather_wait(x_hbm, xbuf.at[nxt], sems.at[nxt], rows)


def _combine_kernel(s0_ref, s1_ref, s0n_ref, s1n_ref, yb_hbm, x2_ref, route_ref, gfin_ref, out_ref, ybuf, sems):
    i = pl.program_id(0)
    nb = pl.num_programs(0)
    rows = x2_ref.shape[0]
    slot = i % 2

    def start(a_ref, b_ref, s):
        _row_gather_start(yb_hbm, a_ref, ybuf.at[s, 0], sems.at[s, 0], rows)
        _row_gather_start(yb_hbm, b_ref, ybuf.at[s, 1], sems.at[s, 1], rows)

    @pl.when(i == 0)
    def _():
        start(s0_ref, s1_ref, 0)

    @pl.when(i + 1 < nb)
    def _():
        start(s0n_ref, s1n_ref, 1 - slot)

    _row_gather_wait(yb_hbm, ybuf.at[slot, 0], sems.at[slot, 0], rows)
    _row_gather_wait(yb_hbm, ybuf.at[slot, 1], sems.at[slot, 1], rows)
    route = route_ref[...]
    g0 = route[:, ROUTE_GATE0:ROUTE_GATE0 + 1]
    g1 = route[:, ROUTE_GATE1:ROUTE_GATE1 + 1]
    y = jnp.concatenate([g0 * ybuf[slot, 0, :, s, :] + g1 * ybuf[slot, 1, :, s, :] for s in range(ybuf.shape[3])],
                        axis=1)
    out_ref[...] = _rms(x2_ref[...] + y, gfin_ref[...])


def _moe_and_final(x2, u3, route, w_e1, w_e3, w_e2, g_final, tm, blk):
    n, d_model = x2.shape
    n_slab = d_model // LANES
    n_exp = w_e1.shape[0]
    a_total = n * EXPERT_TOPK
    n_blocks = -(-(a_total + n_exp * (blk - 1)) // blk)
    p_rows = n_blocks * blk

    fe = route[:, ROUTE_EXPERT0:ROUTE_EXPERT1 + 1].astype(I32).reshape(-1)
    onehot = (fe[:, None] == jnp.arange(n_exp, dtype=I32)[None, :]).astype(I32)
    csum = jnp.cumsum(onehot, axis=0)
    rank = jnp.sum((csum - onehot) * onehot, axis=1)
    counts = csum[-1]
    padded = (counts + blk - 1) // blk * blk
    pad_end = jnp.cumsum(padded)
    dest = (pad_end - padded)[fe] + rank
    ft = jnp.arange(a_total, dtype=I32) // EXPERT_TOPK
    slot_tok = jnp.zeros((p_rows,), I32).at[dest].set(ft)
    block_start = jnp.arange(n_blocks, dtype=I32) * blk
    block_exp = jnp.minimum(jnp.sum((pad_end[None, :] <= block_start[:, None]).astype(I32), axis=1), n_exp - 1)

    tok3 = slot_tok.reshape(n_blocks, 1, blk)
    smem_blk = lambda w, f: pl.BlockSpec((1, 1, w), f, memory_space=pltpu.SMEM)
    w_spec = lambda shp: pl.BlockSpec((1,) + shp[1:], lambda i, be: (be[i], 0, 0))
    yb = pl.pallas_call(
        _expert_kernel,
        grid_spec=pltpu.PrefetchScalarGridSpec(
            num_scalar_prefetch=1,
            grid=(n_blocks,),
            in_specs=[smem_blk(blk, lambda i, be: (i, 0, 0)),
                      smem_blk(blk, lambda i, be: (jnp.minimum(i + 1, n_blocks - 1), 0, 0)),
                      pl.BlockSpec(memory_space=pl.ANY),
                      w_spec(w_e1.shape), w_spec(w_e3.shape), w_spec(w_e2.shape)],
            out_specs=pl.BlockSpec((blk, n_slab, LANES), lambda i, be: (i, 0, 0)),
            scratch_shapes=[pltpu.VMEM((2, blk, n_slab, LANES), F32), pltpu.VMEM((blk, d_model), F32),
                            pltpu.SemaphoreType.DMA((2,))],
        ),
        out_shape=jax.ShapeDtypeStruct((p_rows, n_slab, LANES), F32),
        compiler_params=_cparams(("arbitrary",)),
        name="moe_experts",
    )(block_exp, tok3, tok3, u3, w_e1.astype(BF16), w_e3.astype(BF16), w_e2.astype(BF16))

    tm = min(tm, n)
    assert n % tm == 0
    nt = n // tm
    d2 = dest.reshape(n, EXPERT_TOPK)
    s0 = d2[:, 0].reshape(nt, 1, tm)
    s1 = d2[:, 1].reshape(nt, 1, tm)
    cur = lambda i: (i, 0, 0)
    nxt = lambda i: (jnp.minimum(i + 1, nt - 1), 0, 0)
    smem_blk2 = lambda f: pl.BlockSpec((1, 1, tm), f, memory_space=pltpu.SMEM)
    row_spec = lambda w: pl.BlockSpec((tm, w), lambda i: (i, 0))
    return pl.pallas_call(
        _combine_kernel,
        grid=(nt,),
        in_specs=[smem_blk2(cur), smem_blk2(cur), smem_blk2(nxt), smem_blk2(nxt),
                  pl.BlockSpec(memory_space=pl.ANY), row_spec(d_model), row_spec(LANES), _const_spec((1, d_model))],
        out_specs=row_spec(d_model),
        out_shape=jax.ShapeDtypeStruct((n, d_model), F32),
        scratch_shapes=[pltpu.VMEM((2, EXPERT_TOPK, tm, n_slab, LANES), F32),
                        pltpu.SemaphoreType.DMA((2, EXPERT_TOPK))],
        compiler_params=_cparams(("arbitrary",)),
        name="moe_combine",
    )(s0, s1, s0, s1, yb, x2, route, g_final.reshape(1, d_model))


def _layer(x, pos, conv_buf, h0, attend, mk, mv, lw, g_final, tiles):
    batch, t_len, d_model = x.shape
    n = batch * t_len
    d_lru = lw['w_conv'].shape[1]
    d_att = N_HEADS * HEAD_DIM
    d_idx = IDX_HEADS * IDX_DIM
    x2d = x.reshape(n, d_model)
    xl, q, k, v, qi, kiwi, gl, ga = _in_proj(x2d, lw['g_mix'], lw['w_in_packed'], pos, t_len, d_lru, d_att, d_idx,
                                             tiles['in_proj'])
    att = attend(qi, q, kiwi, k, v)
    hs, h_last, new_buf = _rglru(xl.reshape(batch, t_len, d_lru), conv_buf, h0, lw['w_conv'], lw['b_conv'],
                                 lw['w_ra'], lw['b_ra'], lw['w_ri'], lw['b_ri'], lw['lru_lambda'],
                                 tiles['lru_t'], tiles['lru_b'])
    x1, qc = _mix(x2d, hs.reshape(n, d_lru), att, gl, ga, lw['w_lru_out'], lw['w_att_out'], lw['w_mix_out'],
                  lw['g_cross'], lw['w_cq'], tiles['mix'])
    x2, u3, route = _cross(x1, qc, mk, mv, lw['w_co'], lw['g_ffn'], lw['w_router'], lw['b_router'], batch, t_len,
                           tiles['cross'], tiles['cross_seqs'])
    y = _moe_and_final(x2, u3, route, lw['w_e1'], lw['w_e3'], lw['w_e2'], g_final, tiles['combine'],
                       tiles['moe_rows'])
    k5 = k.reshape(batch, t_len, N_HEADS, HEAD_DIM)
    v5 = v.reshape(batch, t_len, N_HEADS, HEAD_DIM)
    ki = kiwi[:, :IDX_DIM].reshape(batch, t_len, IDX_DIM)
    return y.reshape(batch, t_len, d_model), new_buf, h_last, k5, v5, ki


def kernel(x_prompt, mem_prompt, x_sample, cache_k, cache_v, cache_kidx, cache_mem_k, cache_mem_v, state_conv, state_lru, page_table, g_mix, w_in, w_conv, b_conv, w_ra, b_ra, w_ri, b_ri, lru_lambda, w_lru_out, w_att_out, w_mix_out, g_cross, g_mem, w_cq, w_mk, w_mv, w_co, g_ffn, w_rg, b_rg, w_re, b_re, w_e1, w_e3, w_e2, g_final):
    depth = w_in.shape[0]
    assert depth == 1, "the final norm is fused into the last layer's MoE combine; one layer supported"
    l = 0
    b_p, t_p, _ = x_prompt.shape
    b_s, t_s, _ = x_sample.shape
    past = page_table.shape[1] * cache_k.shape[2]
    d_lru = w_conv.shape[2]
    d_att = N_HEADS * HEAD_DIM
    d_idx = IDX_HEADS * IDX_DIM
    w_router, b_router = _pack_router(w_rg[l], b_rg[l], w_re[l], b_re[l])
    lw = {
        'g_mix': g_mix[l], 'w_in_packed': _pack_w_in(w_in[l], d_lru, d_att, d_idx),
        'w_conv': w_conv[l], 'b_conv': b_conv[l], 'w_ra': w_ra[l], 'b_ra': b_ra[l], 'w_ri': w_ri[l], 'b_ri': b_ri[l],
        'lru_lambda': lru_lambda[l], 'w_lru_out': w_lru_out[l], 'w_att_out': w_att_out[l], 'w_mix_out': w_mix_out[l],
        'g_cross': g_cross[l], 'w_cq': w_cq[l], 'w_co': w_co[l], 'g_ffn': g_ffn[l],
        'w_router': w_router, 'b_router': b_router, 'w_e1': w_e1[l], 'w_e3': w_e3[l], 'w_e2': w_e2[l],
    }

    mk_p, mv_p, mkb_p, mvb_p = _mem_kv(mem_prompt, g_mem[l], w_mk[l], w_mv[l])
    attend_p = functools.partial(_prompt_attention, batch=b_p, t_len=t_p)
    tiles_p = {'in_proj': 512, 'lru_t': 128, 'lru_b': 8, 'mix': 512, 'cross': 512, 'cross_seqs': 1,
               'combine': 256, 'moe_rows': 256}
    y_p, buf_p, h_p, k_p, v_p, ki_p = _layer(
        x_prompt, jnp.arange(t_p), jnp.zeros((b_p, CONV_W - 1, d_lru), F32), jnp.zeros((b_p, d_lru), F32),
        attend_p, mkb_p, mvb_p, lw, g_final, tiles_p)

    def attend_s(qi, q, kiwi, k, v):
        return _sample_attention(qi, q, kiwi, k, v, cache_k, cache_v, cache_kidx, page_table, l, t_s,
                                 pg_sel=16, pg_att=16)
    tiles_s = {'in_proj': 512, 'lru_t': 8, 'lru_b': 8, 'mix': 512, 'cross': t_s, 'cross_seqs': 8,
               'combine': 256, 'moe_rows': 128}
    flat_mem = lambda m: m.reshape(m.shape[0], m.shape[1], -1).astype(BF16)
    y_s, buf_s, h_s, k_s, v_s, ki_s = _layer(
        x_sample, past + jnp.arange(t_s), state_conv[l], state_lru[l], attend_s,
        flat_mem(cache_mem_k[l]), flat_mem(cache_mem_v[l]), lw, g_final, tiles_s)

    st = lambda a: a[None]
    return (y_p, y_s, st(k_p), st(v_p), st(ki_p), st(buf_p), st(h_p), st(mk_p), st(mv_p),
            st(k_s), st(v_s), st(ki_s), st(buf_s), st(h_s))
```

```python
import functools
import math

import jax
import jax.numpy as jnp
import numpy as np
from jax import lax
from jax.experimental import pallas as pl
from jax.experimental.pallas import tpu as pltpu

F32 = jnp.float32
BF16 = jnp.bfloat16
I32 = jnp.int32
I16 = jnp.int16
HALF = 1 << 15

N_HEADS = 8
HEAD_DIM = 64
IDX_HEADS = 8
IDX_DIM = 64
TOPK_MAX = 256
Q_BLOCK = 128
ROPE_THETA = 10000.0
CONV_W = 4
LRU_C = 8.0
LRU_BLOCKS = 8
C_HEADS = 4
N_GROUPS = 4
EXPERTS_PER_GROUP = 8
EXPERT_TOPK = 2
MOE_BLOCK = 128
NORM_EPS = 1e-6

LANES = 128
VMEM_LIMIT = 56 * 1024 * 1024


def _cparams(sem):
    return pltpu.CompilerParams(dimension_semantics=sem, vmem_limit_bytes=VMEM_LIMIT)


def _const_spec(shape):
    nd = len(shape)
    return pl.BlockSpec(shape, lambda *_: (0,) * nd)


def _rms(x, g):
    return x * lax.rsqrt(jnp.mean(x * x, axis=-1, keepdims=True) + NORM_EPS) * g


def _rope_rot(x, cos, sin_signed):
    w = x.shape[1]
    reps = w // LANES
    fwd = pltpu.roll(x, 32, axis=1)
    bwd = pltpu.roll(x, w - 32, axis=1)
    lane = lax.broadcasted_iota(I32, x.shape, 1)
    rot = jnp.where((lane % 64) < 32, bwd, fwd)
    if reps > 1:
        cos = jnp.concatenate([cos] * reps, axis=1)
        sin_signed = jnp.concatenate([sin_signed] * reps, axis=1)
    return x * cos + rot * sin_signed


def _in_proj_kernel(x_ref, g_ref, w_ref, cos_ref, sin_ref,
                    xl_ref, q_ref, k_ref, v_ref, qi_ref, kiwi_ref, gl_ref, ga_ref, *, d_lru, d_att, d_idx, d_model):
    u = _rms(x_ref[...], g_ref[...]).astype(BF16)
    cos = cos_ref[...]
    sin = sin_ref[...]

    def proj(c0, width):
        return jnp.dot(u, w_ref[:, c0:c0 + width], preferred_element_type=F32)

    c = 0
    xl_ref[...] = proj(c, d_lru); c += d_lru
    q_ref[...] = (_rope_rot(proj(c, d_att), cos, sin) * (HEAD_DIM ** -0.5)).astype(BF16); c += d_att
    k_ref[...] = _rope_rot(proj(c, d_att), cos, sin); c += d_att
    v_ref[...] = proj(c, d_att); c += d_att
    qi_ref[...] = (_rope_rot(proj(c, d_idx), cos, sin) * (IDX_DIM ** -0.5)).astype(BF16); c += d_idx
    kw = proj(c, LANES); c += LANES
    lane = lax.broadcasted_iota(I32, kw.shape, 1)
    kiwi_ref[...] = jnp.where(lane < IDX_DIM, _rope_rot(kw, cos, sin), kw * (IDX_HEADS ** -0.5))
    gl_ref[...] = jax.nn.sigmoid(proj(c, d_model)); c += d_model
    ga_ref[...] = jax.nn.sigmoid(proj(c, d_model))


def _rope_tables(pos, rows):
    inv = ROPE_THETA ** (-jnp.arange(0, HEAD_DIM, 2, dtype=F32) / HEAD_DIM)
    ang = pos.astype(F32)[:, None] * inv[None, :]
    cos = jnp.cos(ang)
    sin = jnp.sin(ang)
    cos_t = jnp.concatenate([cos, cos, cos, cos], axis=1)
    sin_t = jnp.concatenate([-sin, sin, -sin, sin], axis=1)
    reps = rows // pos.shape[0]
    if reps > 1:
        cos_t = jnp.tile(cos_t, (reps, 1))
        sin_t = jnp.tile(sin_t, (reps, 1))
    return cos_t, sin_t


def _in_proj(x2d, g, w_packed, pos, seq_len, d_lru, d_att, d_idx, tm):
    n, d_model = x2d.shape
    tm = min(tm, n)
    assert n % tm == 0
    rows = max(seq_len, tm)
    assert rows % tm == 0 and (tm % seq_len == 0 or seq_len % tm == 0)
    cos_t, sin_t = _rope_tables(pos, rows)
    nt = rows // tm
    row_spec = lambda w: pl.BlockSpec((tm, w), lambda i: (i, 0))
    tab_spec = pl.BlockSpec((tm, LANES), lambda i: (i % nt, 0))
    kern = functools.partial(_in_proj_kernel, d_lru=d_lru, d_att=d_att, d_idx=d_idx, d_model=d_model)
    out_shape = (
        jax.ShapeDtypeStruct((n, d_lru), F32),
        jax.ShapeDtypeStruct((n, d_att), BF16),
        jax.ShapeDtypeStruct((n, d_att), F32),
        jax.ShapeDtypeStruct((n, d_att), F32),
        jax.ShapeDtypeStruct((n, d_idx), BF16),
        jax.ShapeDtypeStruct((n, LANES), F32),
        jax.ShapeDtypeStruct((n, d_model), F32),
        jax.ShapeDtypeStruct((n, d_model), F32),
    )
    return pl.pallas_call(
        kern,
        grid=(n // tm,),
        in_specs=[row_spec(d_model), _const_spec((1, d_model)), _const_spec(w_packed.shape), tab_spec, tab_spec],
        out_specs=tuple(row_spec(s.shape[1]) for s in out_shape),
        out_shape=out_shape,
        compiler_params=_cparams(("parallel",)),
        name="in_proj",
    )(x2d, g.reshape(1, d_model), w_packed, cos_t, sin_t)


def _pack_w_in(w_in, d_lru, d_att, d_idx):
    d_model = w_in.shape[0]
    c = d_lru + 3 * d_att + d_idx
    kiwi = w_in[:, c:c + IDX_DIM + IDX_HEADS]
    pad = jnp.zeros((d_model, LANES - IDX_DIM - IDX_HEADS), w_in.dtype)
    return jnp.concatenate([w_in[:, :c], kiwi, pad, w_in[:, c + IDX_DIM + IDX_HEADS:]], axis=1).astype(BF16)


CONV_PAD = 8


def _softplus(x):
    return jnp.maximum(x, 0.0) + jnp.log1p(jnp.exp(-jnp.abs(x)))


def _rglru_kernel(xl_ref, buf0_ref, h0_ref, wconv_ref, bconv_ref, wra_ref, bra_ref, wri_ref, bri_ref, lam_ref,
                  hs_ref, hlast_ref, newbuf_ref, xpad_scr, a_scr, u_scr, h_scr, *, tc, bb):
    j = pl.program_id(1)
    d = xl_ref.shape[-1]
    hist = CONV_W - 1

    @pl.when(j == 0)
    def _():
        xpad_scr[:, 0:CONV_PAD - hist, :] = jnp.zeros((bb, CONV_PAD - hist, d), F32)
        xpad_scr[:, CONV_PAD - hist:CONV_PAD, :] = buf0_ref[...]
        h_scr[...] = h0_ref[...]

    xpad_scr[:, CONV_PAD:CONV_PAD + tc, :] = xl_ref[...]
    xc = jnp.zeros((bb, tc, d), F32) + bconv_ref[...]
    for i in range(CONV_W):
        off = CONV_PAD - hist + i
        xc = xc + xpad_scr[:, off:off + tc, :] * wconv_ref[i:i + 1, :]
    newbuf_ref[...] = xpad_scr[:, CONV_PAD + tc - hist:CONV_PAD + tc, :]
    xpad_scr[:, 0:CONV_PAD, :] = xpad_scr[:, tc:tc + CONV_PAD, :]

    xc2 = xc.reshape(bb * tc, d)
    xb = xc2.astype(BF16)
    r = jax.nn.sigmoid(jnp.dot(xb, wra_ref[...], preferred_element_type=F32) + bra_ref[...])
    g = jax.nn.sigmoid(jnp.dot(xb, wri_ref[...], preferred_element_type=F32) + bri_ref[...])
    log_a = (-LRU_C) * r * _softplus(-lam_ref[...])
    a = jnp.exp(log_a)
    mult = jnp.sqrt(-jnp.tanh(log_a) * (a * a + 1.0))
    a_scr[...] = a.reshape(bb, tc, d)
    u_scr[...] = (mult * (g * xc2)).reshape(bb, tc, d)

    def step(t, hs):
        new = []
        for b in range(bb):
            h = a_scr[b, pl.ds(t, 1), :] * hs[b] + u_scr[b, pl.ds(t, 1), :]
            u_scr[b, pl.ds(t, 1), :] = h
            new.append(h)
        return tuple(new)

    h_fin = lax.fori_loop(0, tc, step, tuple(h_scr[b] for b in range(bb)))
    for b in range(bb):
        h_scr[b] = h_fin[b]
        hlast_ref[b] = h_fin[b]
    hs_ref[...] = u_scr[...].astype(hs_ref.dtype)


def _block_diag(w):
    nb, c, _ = w.shape
    eye = jnp.eye(nb, dtype=w.dtype)
    return (eye[:, None, :, None] * w[:, :, None, :]).reshape(nb * c, nb * c)


def _rglru(xl, buf0, h0, w_conv, b_conv, w_ra, b_ra, w_ri, b_ri, lam, tc, bb):
    b, t, d = xl.shape
    tc = min(tc, t)
    bb = min(bb, b)
    assert t % tc == 0 and b % bb == 0 and tc % 8 == 0
    row = lambda a: a.reshape(1, d)
    kern = functools.partial(_rglru_kernel, tc=tc, bb=bb)
    hist = CONV_W - 1
    out_shape = (
        jax.ShapeDtypeStruct((b, t, d), BF16 if tc % 16 == 0 else F32),
        jax.ShapeDtypeStruct((b, 1, d), F32),
        jax.ShapeDtypeStruct((b, hist, d), F32),
    )
    hs, h_last, new_buf = pl.pallas_call(
        kern,
        grid=(b // bb, t // tc),
        in_specs=[
            pl.BlockSpec((bb, tc, d), lambda i, j: (i, j, 0)),
            pl.BlockSpec((bb, hist, d), lambda i, j: (i, 0, 0)),
            pl.BlockSpec((bb, 1, d), lambda i, j: (i, 0, 0)),
            _const_spec((CONV_W, d)), _const_spec((1, d)),
            _const_spec((d, d)), _const_spec((1, d)), _const_spec((d, d)), _const_spec((1, d)), _const_spec((1, d)),
        ],
        out_specs=(
            pl.BlockSpec((bb, tc, d), lambda i, j: (i, j, 0)),
            pl.BlockSpec((bb, 1, d), lambda i, j: (i, 0, 0)),
            pl.BlockSpec((bb, hist, d), lambda i, j: (i, 0, 0)),
        ),
        out_shape=out_shape,
        scratch_shapes=[
            pltpu.VMEM((bb, tc + CONV_PAD, d), F32),
            pltpu.VMEM((bb, tc, d), F32),
            pltpu.VMEM((bb, tc, d), F32),
            pltpu.VMEM((bb, 1, d), F32),
        ],
        compiler_params=_cparams(("parallel", "arbitrary")),
        name="rglru",
    )(xl, buf0, h0.reshape(b, 1, d), w_conv, row(b_conv),
      _block_diag(w_ra).astype(BF16), row(b_ra), _block_diag(w_ri).astype(BF16), row(b_ri), row(lam))
    return hs, h_last.reshape(b, d), new_buf


INT_MIN = -2 ** 31
NEG_INF_KEY = -2139095041
KEY_GROUP = 4
SELECT_SEQS = 8


def _sortable_key(score):
    score = jnp.where(score == 0.0, 0.0, score)
    bits = pltpu.bitcast(score, I32)
    return bits ^ ((bits >> 31) & 0x7FFFFFFF)


def _pair_block_diag(x):
    lane = lax.broadcasted_iota(I32, x.shape, 1)
    zero = jnp.zeros_like(x)
    return jnp.concatenate([jnp.where(lane < 64, x, zero), jnp.where(lane >= 64, x, zero)], axis=0)


def _dot_nt(a, b):
    return lax.dot_general(a, b, (((1,), (1,)), ((), ())), preferred_element_type=F32)


def _selection_bias(kk, eq, tie_rank, theta, need):
    take = jnp.where(eq, tie_rank, jnp.inf) <= need
    b = jnp.where(take, 0.0, -jnp.inf)
    b = jnp.where(kk > theta, 0.0, b)
    return jnp.where(kk == NEG_INF_KEY, -jnp.inf, b)


def _prompt_attn_kernel(qi_ref, q_ref, kiwi_ref, k_ref, v_ref, att_ref,
                        kdup_scr, kbf_scr, vT_scr, qibd_scr, qbd_scr, key_scr, hi_scr, lo_scr, bias_scr, logit_scr,
                        oT_scr, acc_scr,
                        *, topk, qb):
    j = pl.program_id(1)
    t_len = k_ref.shape[0]
    n_pairs = q_ref.shape[1] // LANES
    nk = j + 1

    def rows_of(c):
        return pl.ds(pl.multiple_of(c * qb, qb), qb)

    @pl.when(j == 0)
    def _():
        def prep(c, carry):
            rows = rows_of(c)
            kw = kiwi_ref[rows, :]
            lane = lax.broadcasted_iota(I32, kw.shape, 1)
            kdup_scr[rows, :] = jnp.where(lane < IDX_DIM, kw, pltpu.roll(kw, IDX_DIM, axis=1)).astype(BF16)
            kbf_scr[rows, :] = k_ref[rows, :].astype(BF16)
            vT_scr[:, rows] = v_ref[rows, :].T.astype(BF16)
            return carry
        lax.fori_loop(0, t_len // qb, prep, 0)

    for p in range(n_pairs):
        qibd_scr[p] = _pair_block_diag(qi_ref[:, p * LANES:(p + 1) * LANES])
        qbd_scr[p] = _pair_block_diag(q_ref[:, p * LANES:(p + 1) * LANES])
    w_t = kiwi_ref[rows_of(j), :].T

    def key_loop(body, init):
        wide = KEY_GROUP * qb
        n_groups = (nk + KEY_GROUP - 1) // KEY_GROUP
        return lax.fori_loop(0, n_groups, lambda i, c: body(pl.multiple_of(i * wide, wide), wide, c), init)

    def idx_body(r0, nr, carry):
        rows = pl.ds(r0, nr)
        kd = kdup_scr[rows, :]
        sc = jnp.zeros((nr, qb), F32)
        for p in range(n_pairs):
            s2 = jnp.maximum(_dot_nt(kd, qibd_scr[p]), 0.0)
            h = IDX_DIM + 2 * p
            sc = sc + w_t[h:h + 1, :] * s2[:, :qb] + w_t[h + 1:h + 2, :] * s2[:, qb:]
        kpos = r0 + lax.broadcasted_iota(I32, (nr, qb), 0)
        qpos = j * qb + lax.broadcasted_iota(I32, (nr, qb), 1)
        key = _sortable_key(jnp.where(kpos <= qpos, sc, -jnp.inf))
        key_scr[rows, :] = key
        hi_scr[rows, :] = (key >> 16).astype(I16)
        lo_scr[rows, :] = ((key & 0xFFFF) - HALF).astype(I16)
        return carry
    key_loop(idx_body, 0)

    def count16(ref, pred):
        def body(r0, nr, acc):
            hit = jnp.where(pred(ref[pl.ds(r0, nr), :]), jnp.int16(1), jnp.int16(0))
            parts = [hit[s * 16:(s + 1) * 16, :] for s in range(nr // 16)]
            while len(parts) > 1:
                parts = [a + b for a, b in zip(parts[0::2], parts[1::2])]
            return acc + parts[0]
        acc = key_loop(body, jnp.zeros((16, qb), I16))
        return jnp.sum(acc.astype(I32), axis=0, keepdims=True)

    def search16(ref, base):
        def bit_step(i, prefix):
            cand_u = prefix | jnp.left_shift(jnp.int32(1), 15 - i)
            cand_s = (cand_u - HALF).astype(I16)
            return jnp.where(base + count16(ref, lambda v: v >= cand_s) >= topk, cand_u, prefix)
        return lax.fori_loop(0, 16, bit_step, jnp.zeros((1, qb), I32))

    hi_u = search16(hi_scr, jnp.zeros((1, qb), I32))
    hi_s = (hi_u - HALF).astype(I16)
    above = count16(hi_scr, lambda v: v > hi_s)

    def mark_body(r0, nr, carry):
        rows = pl.ds(r0, nr)
        lo_scr[rows, :] = jnp.where(hi_scr[rows, :] == hi_s, lo_scr[rows, :], jnp.int16(-HALF))
        return carry
    key_loop(mark_body, 0)
    lo_u = search16(lo_scr, above)
    theta = jnp.left_shift(hi_u - HALF, 16) + lo_u

    def count(pred):
        def body(r0, nr, acc):
            hit = jnp.where(pred(key_scr[pl.ds(r0, nr), :]), 1, 0).astype(I32)
            return acc + jnp.sum(hit.reshape(nr // 8, 8, qb), axis=0)
        return jnp.sum(key_loop(body, jnp.zeros((8, qb), I32)), axis=0, keepdims=True)

    need = (topk - count(lambda kk: kk > theta)).astype(F32)
    tri = jnp.where(lax.broadcasted_iota(I32, (qb, qb), 0) >= lax.broadcasted_iota(I32, (qb, qb), 1),
                    1.0, 0.0).astype(BF16)

    def mask_body(r0, nr, run):
        for s in range(nr // qb):
            rows = pl.ds(r0 + s * qb, qb)
            kk = key_scr[rows, :]
            eq = kk == theta
            pre = jnp.dot(tri, jnp.where(eq, 1.0, 0.0).astype(BF16), preferred_element_type=F32)
            bias_scr[rows, :] = _selection_bias(kk, eq, run + pre, theta, need)
            run = run + pre[qb - 1:qb, :]
        return run
    key_loop(mask_body, jnp.zeros((1, qb), F32))

    def pair_lanes(p):
        return slice(p * LANES, (p + 1) * LANES)

    def pass1(r0, nr, m8s):
        rows = pl.ds(r0, nr)
        bias = bias_scr[rows, :]
        bias2 = jnp.concatenate([bias, bias], axis=1)
        new = []
        for p in range(n_pairs):
            lg = _dot_nt(kbf_scr[rows, pair_lanes(p)], qbd_scr[p]) + bias2
            logit_scr[p, rows, :] = lg
            new.append(jnp.maximum(m8s[p], jnp.max(lg.reshape(nr // 8, 8, 2 * qb), axis=0)))
        return tuple(new)
    m8s = key_loop(pass1, tuple(jnp.full((8, 2 * qb), -jnp.inf, F32) for _ in range(n_pairs)))
    ms = [jnp.max(m8, axis=0, keepdims=True) for m8 in m8s]

    acc_scr[...] = jnp.zeros(acc_scr.shape, F32)

    def pass2(r0, nr, l8s):
        rows = pl.ds(r0, nr)
        new = []
        for p in range(n_pairs):
            pe = jnp.exp(logit_scr[p, rows, :] - ms[p])
            new.append(l8s[p] + jnp.sum(pe.reshape(nr // 8, 8, 2 * qb), axis=0))
            acc_scr[p] += jnp.dot(vT_scr[pair_lanes(p), rows], pe.astype(BF16), preferred_element_type=F32)
        return tuple(new)
    l8s = key_loop(pass2, tuple(jnp.zeros((8, 2 * qb), F32) for _ in range(n_pairs)))

    for p in range(n_pairs):
        o = acc_scr[p] / jnp.sum(l8s[p], axis=0, keepdims=True)
        oT_scr[p * LANES:p * LANES + HEAD_DIM, :] = o[:HEAD_DIM, :qb]
        oT_scr[p * LANES + HEAD_DIM:(p + 1) * LANES, :] = o[HEAD_DIM:, qb:]
    att_ref[...] = oT_scr[...].T.astype(att_ref.dtype)


def _prompt_attention(qi, q, kiwi, k, v, batch, t_len):
    n, d_att = q.shape
    qb = min(Q_BLOCK, t_len)
    assert qb == LANES and t_len % (KEY_GROUP * qb) == 0
    nqb = t_len // qb
    topk = min(TOPK_MAX, t_len // 4)
    n_pairs = d_att // LANES
    kern = functools.partial(_prompt_attn_kernel, topk=topk, qb=qb)
    blk_spec = lambda w: pl.BlockSpec((qb, w), lambda b, j: (b * nqb + j, 0))
    seq_spec = lambda w: pl.BlockSpec((t_len, w), lambda b, j: (b, 0))
    return pl.pallas_call(
        kern,
        grid=(batch, nqb),
        in_specs=[blk_spec(qi.shape[1]), blk_spec(d_att), seq_spec(LANES), seq_spec(d_att), seq_spec(d_att)],
        out_specs=blk_spec(d_att),
        out_shape=jax.ShapeDtypeStruct((n, d_att), BF16),
        scratch_shapes=[
            pltpu.VMEM((t_len, LANES), BF16),
            pltpu.VMEM((t_len, d_att), BF16),
            pltpu.VMEM((d_att, t_len), BF16),
            pltpu.VMEM((n_pairs, 2 * qb, LANES), BF16),
            pltpu.VMEM((n_pairs, 2 * qb, LANES), BF16),
            pltpu.VMEM((t_len, qb), I32),
            pltpu.VMEM((t_len, qb), I16),
            pltpu.VMEM((t_len, qb), I16),
            pltpu.VMEM((t_len, qb), F32),
            pltpu.VMEM((n_pairs, t_len, 2 * qb), F32),
            pltpu.VMEM((d_att, qb), F32),
            pltpu.VMEM((n_pairs, LANES, 2 * qb), F32),
        ],
        compiler_params=_cparams(("parallel", "arbitrary")),
        name="prompt_attn",
    )(qi, q, kiwi, k, v)


def _sample_score_kernel(pt_ref, qi_ref, wi_ref, kinew_ref, *rest, pg, past, page, t_new):
    del pt_ref
    page_refs = rest[:pg]
    key_ref = rest[pg]
    g = pl.program_id(1)
    qi = qi_ref[...]
    wi = wi_ref[...]

    def head_sum(s):
        s = jnp.maximum(s, 0.0) * wi
        return jnp.sum(s.reshape(IDX_HEADS, t_new, s.shape[1]), axis=0)

    k_t = jnp.concatenate([r[0, 0] for r in page_refs], axis=1).astype(BF16)
    off = pl.multiple_of(g * (pg * page), page)
    key_ref[0, :, pl.ds(off, pg * page)] = _sortable_key(head_sum(jnp.dot(qi, k_t, preferred_element_type=F32)))

    @pl.when(g == pl.num_programs(1) - 1)
    def _():
        knew = jnp.concatenate([kinew_ref[:, :IDX_DIM], jnp.zeros((page - t_new, IDX_DIM), F32)], axis=0)
        sc = head_sum(_dot_nt(qi, knew.astype(BF16)))
        tok = lax.broadcasted_iota(I32, sc.shape, 0)
        kk_i = lax.broadcasted_iota(I32, sc.shape, 1)
        key_ref[0, :, past:past + page] = _sortable_key(jnp.where(kk_i <= tok, sc, -jnp.inf))


def _sample_threshold_kernel(key_ref, bias_ref, *, topk, page):
    n_seq, t_new, l_pad = key_ref.shape
    rows = n_seq * t_new
    n_chunks = l_pad // page

    def chunk(c):
        return key_ref[:, :, c * page:(c + 1) * page].reshape(rows, page)

    def count(pred):
        acc = jnp.zeros((rows, page), I32)
        for c in range(n_chunks):
            acc = acc + jnp.where(pred(chunk(c)), 1, 0).astype(I32)
        return jnp.sum(acc, axis=1, keepdims=True)

    def bit_step(i, prefix):
        cand_u = prefix | jnp.left_shift(jnp.int32(1), 31 - i)
        cand_s = cand_u ^ INT_MIN
        return jnp.where(count(lambda kk: kk >= cand_s) >= topk, cand_u, prefix)
    theta = lax.fori_loop(0, 32, bit_step, jnp.zeros((rows, 1), I32)) ^ INT_MIN
    need = (topk - count(lambda kk: kk > theta)).astype(F32)

    row_i = lax.broadcasted_iota(I32, (page, page), 0)
    col_i = lax.broadcasted_iota(I32, (page, page), 1)
    triu = jnp.where(row_i <= col_i, 1.0, 0.0).astype(BF16)
    run = jnp.zeros((rows, 1), F32)
    for c in range(n_chunks):
        kk = chunk(c)
        eq = kk == theta
        pre = jnp.dot(jnp.where(eq, 1.0, 0.0).astype(BF16), triu, preferred_element_type=F32)
        bias = _selection_bias(kk, eq, run + pre, theta, need)
        bias_ref[:, :, c * page:(c + 1) * page] = bias.reshape(n_seq, t_new, page)
        run = run + pre[:, page - 1:page]


def _sample_attn_kernel(pt_ref, q_ref, bias_ref, knew_ref, vnew_ref, *rest, pg, past, page, t_new):
    del pt_ref
    k_refs = rest[:pg]
    v_refs = rest[pg:2 * pg]
    out_ref = rest[2 * pg]
    m_scr, l_scr, acc_scr = rest[2 * pg + 1:]
    g = pl.program_id(1)
    qbd = q_ref[...]

    @pl.when(g == 0)
    def _():
        m_scr[...] = jnp.full(m_scr.shape, -jnp.inf, F32)
        l_scr[...] = jnp.zeros(l_scr.shape, F32)
        acc_scr[...] = jnp.zeros(acc_scr.shape, F32)

    def attend(lg, bias, pv_fn):
        lg = lg + jnp.concatenate([bias] * N_HEADS, axis=0)
        m_old = m_scr[...]
        m_new = jnp.maximum(m_old, jnp.max(lg, axis=1, keepdims=True))
        m_safe = jnp.where(m_new == -jnp.inf, 0.0, m_new)
        alpha = jnp.exp(m_old - m_safe)
        pe = jnp.exp(lg - m_safe)
        l_scr[...] = alpha * l_scr[...] + jnp.sum(pe, axis=1, keepdims=True)
        acc_scr[...] = alpha * acc_scr[...] + pv_fn(pe.astype(BF16))
        m_scr[...] = m_new

    k_t = jnp.concatenate([r[0, 0] for r in k_refs], axis=1).astype(BF16)
    v_t = jnp.concatenate([r[0, 0] for r in v_refs], axis=1).astype(BF16)
    off = pl.multiple_of(g * (pg * page), page)
    attend(jnp.dot(qbd, k_t, preferred_element_type=F32), bias_ref[0, :, pl.ds(off, pg * page)],
           lambda pb: _dot_nt(pb, v_t))

    @pl.when(g == pl.num_programs(1) - 1)
    def _():
        pad = jnp.zeros((page - t_new, knew_ref.shape[1]), F32)
        k_new = jnp.concatenate([knew_ref[...], pad], axis=0).astype(BF16)
        v_new = jnp.concatenate([vnew_ref[...], pad], axis=0).astype(BF16)
        attend(_dot_nt(qbd, k_new), bias_ref[0, :, past:past + page],
               lambda pb: jnp.dot(pb, v_new, preferred_element_type=F32))
        out_ref[0] = acc_scr[...] / l_scr[...]


def _to_head_major(x, batch, t_new, heads, dim):
    return x.reshape(batch, t_new, heads, dim).transpose(0, 2, 1, 3).reshape(batch * heads * t_new, dim)


def _sample_attention(qi, q, kiwi, k_new, v_new, cache_k, cache_v, cache_kidx, page_table, layer, t_new, pg_sel, pg_att):
    batch, n_pages = page_table.shape
    depth, n_pool, page = cache_k.shape[:3]
    past = n_pages * page
    pg_sel = min(pg_sel, n_pages)
    pg_att = min(pg_att, n_pages)
    assert n_pages % pg_sel == 0 and n_pages % pg_att == 0 and page == LANES and t_new == 8
    l_pad = past + page
    topk = min(TOPK_MAX, (past + t_new) // 4)
    rows = IDX_HEADS * t_new
    d_att = N_HEADS * HEAD_DIM

    kidx_t = cache_kidx.transpose(0, 1, 3, 2)
    k_t = cache_k.transpose(0, 1, 3, 4, 2).reshape(depth, n_pool, d_att, page)
    v_t = cache_v.transpose(0, 1, 3, 4, 2).reshape(depth, n_pool, d_att, page)

    qi_hm = _to_head_major(qi, batch, t_new, IDX_HEADS, IDX_DIM)
    wi_col = kiwi[:, IDX_DIM:IDX_DIM + IDX_HEADS].reshape(batch, t_new, IDX_HEADS).transpose(0, 2, 1).reshape(-1, 1)
    seq_spec = lambda r, w: pl.BlockSpec((r, w), lambda b, g, pt: (b, 0))
    bias_spec = pl.BlockSpec((1, t_new, l_pad), lambda b, g, pt: (b, 0, 0))

    def page_spec(arr, i, pg):
        return pl.BlockSpec((1, 1) + arr.shape[2:], lambda b, g, pt: (layer, pt[b, g * pg + i], 0, 0))

    keys = pl.pallas_call(
        functools.partial(_sample_score_kernel, pg=pg_sel, past=past, page=page, t_new=t_new),
        grid_spec=pltpu.PrefetchScalarGridSpec(
            num_scalar_prefetch=1,
            grid=(batch, n_pages // pg_sel),
            in_specs=[seq_spec(rows, IDX_DIM), seq_spec(rows, 1), seq_spec(t_new, LANES)]
                     + [page_spec(kidx_t, i, pg_sel) for i in range(pg_sel)],
            out_specs=bias_spec,
        ),
        out_shape=jax.ShapeDtypeStruct((batch, t_new, l_pad), I32),
        compiler_params=_cparams(("parallel", "arbitrary")),
        name="sample_score",
    )(page_table, qi_hm, wi_col, kiwi, *([kidx_t] * pg_sel))

    sel_seqs = min(SELECT_SEQS, batch)
    assert batch % sel_seqs == 0
    sel_spec = pl.BlockSpec((sel_seqs, t_new, l_pad), lambda b: (b, 0, 0))
    bias = pl.pallas_call(
        functools.partial(_sample_threshold_kernel, topk=topk, page=page),
        grid=(batch // sel_seqs,),
        in_specs=[sel_spec],
        out_specs=sel_spec,
        out_shape=jax.ShapeDtypeStruct((batch, t_new, l_pad), F32),
        compiler_params=_cparams(("parallel",)),
        name="sample_select",
    )(keys)

    q4 = q.reshape(batch, t_new, N_HEADS, HEAD_DIM)
    eye = jnp.eye(N_HEADS, dtype=q.dtype)
    qbd = (q4.transpose(0, 2, 1, 3)[:, :, :, None, :] * eye[None, :, None, :, None]).reshape(batch * rows, d_att)
    out = pl.pallas_call(
        functools.partial(_sample_attn_kernel, pg=pg_att, past=past, page=page, t_new=t_new),
        grid_spec=pltpu.PrefetchScalarGridSpec(
            num_scalar_prefetch=1,
            grid=(batch, n_pages // pg_att),
            in_specs=[seq_spec(rows, d_att), bias_spec, seq_spec(t_new, d_att), seq_spec(t_new, d_att)]
                     + [page_spec(k_t, i, pg_att) for i in range(pg_att)]
                     + [page_spec(v_t, i, pg_att) for i in range(pg_att)],
            out_specs=pl.BlockSpec((1, rows, d_att), lambda b, g, pt: (b, 0, 0)),
            scratch_shapes=[pltpu.VMEM((rows, 1), F32), pltpu.VMEM((rows, 1), F32), pltpu.VMEM((rows, d_att), F32)],
        ),
        out_shape=jax.ShapeDtypeStruct((batch, rows, d_att), F32),
        compiler_params=_cparams(("parallel", "arbitrary")),
        name="sample_attn",
    )(page_table, qbd, bias, k_new, v_new, *([k_t] * pg_att), *([v_t] * pg_att))
    o5 = out.reshape(batch, N_HEADS, t_new, N_HEADS, HEAD_DIM)
    att = jnp.stack([o5[:, h, :, h, :] for h in range(N_HEADS)], axis=2)
    return att.reshape(batch * t_new, d_att).astype(BF16)


def _mix_kernel(x_ref, hs_ref, att_ref, gl_ref, ga_ref, wl_ref, wa_ref, wm_ref, gc_ref, wcq_ref, x1_ref, qc_ref):
    lru = jnp.dot(hs_ref[...].astype(BF16), wl_ref[...], preferred_element_type=F32)
    att = jnp.dot(att_ref[...], wa_ref[...], preferred_element_type=F32)
    mixed = gl_ref[...] * lru + ga_ref[...] * att
    x1 = x_ref[...] + jnp.dot(mixed.astype(BF16), wm_ref[...], preferred_element_type=F32)
    x1_ref[...] = x1
    qc_ref[...] = jnp.dot(_rms(x1, gc_ref[...]).astype(BF16), wcq_ref[...], preferred_element_type=F32)


def _mix(x2d, hs, att, gl, ga, w_lru_out, w_att_out, w_mix_out, g_cross, w_cq, tm):
    n, d_model = x2d.shape
    tm = min(tm, n)
    assert n % tm == 0
    d_cross = w_cq.shape[1]
    row_spec = lambda w: pl.BlockSpec((tm, w), lambda i: (i, 0))
    return pl.pallas_call(
        _mix_kernel,
        grid=(n // tm,),
        in_specs=[row_spec(d_model), row_spec(hs.shape[1]), row_spec(att.shape[1]), row_spec(d_model), row_spec(d_model),
                  _const_spec(w_lru_out.shape), _const_spec(w_att_out.shape), _const_spec(w_mix_out.shape),
                  _const_spec((1, d_model)), _const_spec(w_cq.shape)],
        out_specs=(row_spec(d_model), row_spec(d_cross)),
        out_shape=(jax.ShapeDtypeStruct((n, d_model), F32), jax.ShapeDtypeStruct((n, d_cross), F32)),
        compiler_params=_cparams(("parallel",)),
        name="mix_out",
    )(x2d, hs, att, gl, ga, w_lru_out.astype(BF16), w_att_out.astype(BF16), w_mix_out.astype(BF16),
      g_cross.reshape(1, d_model), w_cq.astype(BF16))


def _mem_kv_kernel(mem_ref, g_ref, wk_ref, wv_ref, mk_ref, mv_ref, mkb_ref, mvb_ref):
    m = _rms(mem_ref[0], g_ref[...]).astype(BF16)
    mk = jnp.dot(m, wk_ref[...], preferred_element_type=F32)
    mv = jnp.dot(m, wv_ref[...], preferred_element_type=F32)
    hd = mk_ref.shape[-1]
    for h in range(C_HEADS):
        mk_ref[0, :, h, :] = mk[:, h * hd:(h + 1) * hd]
        mv_ref[0, :, h, :] = mv[:, h * hd:(h + 1) * hd]
    mkb_ref[0] = mk.astype(BF16)
    mvb_ref[0] = mv.astype(BF16)


def _mem_kv(mem, g_mem, w_mk, w_mv):
    b, s, d_model = mem.shape
    d_cross = w_mk.shape[1]
    hd = d_cross // C_HEADS
    kv_spec = pl.BlockSpec((1, s, C_HEADS, hd), lambda i: (i, 0, 0, 0))
    kv_shape = jax.ShapeDtypeStruct((b, s, C_HEADS, hd), F32)
    flat_spec = pl.BlockSpec((1, s, d_cross), lambda i: (i, 0, 0))
    flat_shape = jax.ShapeDtypeStruct((b, s, d_cross), BF16)
    return pl.pallas_call(
        _mem_kv_kernel,
        grid=(b,),
        in_specs=[pl.BlockSpec((1, s, d_model), lambda i: (i, 0, 0)), _const_spec((1, d_model)),
                  _const_spec(w_mk.shape), _const_spec(w_mv.shape)],
        out_specs=(kv_spec, kv_spec, flat_spec, flat_spec),
        out_shape=(kv_shape, kv_shape, flat_shape, flat_shape),
        compiler_params=_cparams(("parallel",)),
        name="mem_kv",
    )(mem, g_mem.reshape(1, d_model), w_mk.astype(BF16), w_mv.astype(BF16))


ROUTE_EXPERT0, ROUTE_EXPERT1, ROUTE_GATE0, ROUTE_GATE1 = 0, 1, 2, 3


def _lane_min_where(cond, lane):
    return jnp.min(jnp.where(cond, lane, LANES), axis=1, keepdims=True)


def _cross_kernel(x1_ref, qc_ref, mk_ref, mv_ref, wco_ref, gf_ref, wr_ref, br_ref, x2_ref, u3_ref, route_ref):
    n_seq = mk_ref.shape[0]
    tq = x1_ref.shape[0] // n_seq
    hd = mk_ref.shape[-1] // C_HEADS
    scale = hd ** -0.5
    seq_outs = []
    for sq in range(n_seq):
        rows = slice(sq * tq, (sq + 1) * tq)
        outs = []
        for h in range(C_HEADS):
            cols = slice(h * hd, (h + 1) * hd)
            s = _dot_nt(qc_ref[rows, cols].astype(BF16), mk_ref[sq, :, cols]) * scale
            pe = jnp.exp(s - jnp.max(s, axis=1, keepdims=True))
            o = jnp.dot(pe.astype(BF16), mv_ref[sq, :, cols], preferred_element_type=F32)
            outs.append(o / jnp.sum(pe, axis=1, keepdims=True))
        seq_outs.append(jnp.concatenate(outs, axis=1))
    o = jnp.concatenate(seq_outs, axis=0).astype(BF16)
    x2 = x1_ref[...] + jnp.dot(o, wco_ref[...], preferred_element_type=F32)
    x2_ref[...] = x2
    u3 = _rms(x2, gf_ref[...])
    for s in range(u3_ref.shape[1]):
        u3_ref[:, s, :] = u3[:, s * LANES:(s + 1) * LANES]

    logits = jnp.dot(u3.astype(BF16), wr_ref[...], preferred_element_type=F32) + br_ref[...]
    lane = lax.broadcasted_iota(I32, logits.shape, 1)
    lg = jnp.where(lane < N_GROUPS, logits, -jnp.inf)
    g_max = jnp.max(lg, axis=1, keepdims=True)
    g_sel = _lane_min_where(lg == g_max, lane)
    p_grp = 1.0 / jnp.sum(jnp.exp(lg - g_max), axis=1, keepdims=True)
    e_lo = N_GROUPS + g_sel * EXPERTS_PER_GROUP
    le = jnp.where((lane >= e_lo) & (lane < e_lo + EXPERTS_PER_GROUP), logits, -jnp.inf)
    m1 = jnp.max(le, axis=1, keepdims=True)
    i1 = _lane_min_where(le == m1, lane)
    le2 = jnp.where(lane == i1, -jnp.inf, le)
    m2 = jnp.max(le2, axis=1, keepdims=True)
    i2 = _lane_min_where(le2 == m2, lane)
    e2 = jnp.exp(m2 - m1)
    inv = p_grp / (1.0 + e2)
    route = jnp.where(lane == ROUTE_EXPERT0, (i1 - N_GROUPS).astype(F32), 0.0)
    route = jnp.where(lane == ROUTE_EXPERT1, (i2 - N_GROUPS).astype(F32), route)
    route = jnp.where(lane == ROUTE_GATE0, inv, route)
    route_ref[...] = jnp.where(lane == ROUTE_GATE1, inv * e2, route)


def _pack_router(w_rg, b_rg, w_re, b_re):
    d_model = w_rg.shape[0]
    used = w_rg.shape[1] + w_re.shape[1]
    w = jnp.concatenate([w_rg, w_re, jnp.zeros((d_model, LANES - used), w_rg.dtype)], axis=1).astype(BF16)
    b = jnp.concatenate([b_rg, b_re, jnp.zeros((LANES - used,), b_rg.dtype)]).reshape(1, LANES)
    return w, b


def _cross(x1, qc, mk, mv, w_co, g_ffn, w_router, b_router, batch, t_len, tq, n_seq):
    n, d_model = x1.shape
    tq = min(tq, t_len)
    n_seq = min(n_seq, batch)
    assert t_len % tq == 0 and batch % n_seq == 0 and (n_seq == 1 or tq == t_len)
    nq = t_len // tq
    d_cross = qc.shape[1]
    rows = n_seq * tq
    row_spec = lambda w: pl.BlockSpec((rows, w), lambda b, j: (b * nq + j, 0))
    mem_spec = pl.BlockSpec((n_seq,) + mk.shape[1:], lambda b, j: (b, 0, 0))
    return pl.pallas_call(
        _cross_kernel,
        grid=(batch // n_seq, nq),
        in_specs=[row_spec(d_model), row_spec(d_cross), mem_spec, mem_spec, _const_spec(w_co.shape),
                  _const_spec((1, d_model)), _const_spec(w_router.shape), _const_spec((1, LANES))],
        out_specs=(row_spec(d_model), pl.BlockSpec((rows, d_model // LANES, LANES), lambda b, j: (b * nq + j, 0, 0)),
                   row_spec(LANES)),
        out_shape=(jax.ShapeDtypeStruct((n, d_model), F32), jax.ShapeDtypeStruct((n, d_model // LANES, LANES), F32),
                   jax.ShapeDtypeStruct((n, LANES), F32)),
        compiler_params=_cparams(("parallel", "parallel")),
        name="cross_router",
    )(x1, qc, mk, mv, w_co.astype(BF16), g_ffn.reshape(1, d_model), w_router, b_router)


DMA_PRIORITIES = 2


def _row_gather_start(src_hbm, idx_ref, dst, sem, n_rows):
    def issue(i, carry):
        for u in range(DMA_PRIORITIES):
            r = i * DMA_PRIORITIES + u
            pltpu.make_async_copy(src_hbm.at[pl.ds(idx_ref[0, 0, r], 1)], dst.at[pl.ds(r, 1)], sem).start(priority=u)
        return carry
    lax.fori_loop(0, n_rows // DMA_PRIORITIES, issue, 0)


def _row_gather_wait(src_hbm, dst, sem, n_rows):
    pltpu.make_async_copy(src_hbm.at[pl.ds(0, n_rows)], dst, sem).wait()


def _expert_kernel(be_ref, tok_ref, tok_next_ref, x_hbm, w1_ref, w3_ref, w2_ref, y_ref, xbuf, sems):
    del be_ref
    i = pl.program_id(0)
    nb = pl.num_programs(0)
    rows = xbuf.shape[1]
    slot = i % 2

    @pl.when(i == 0)
    def _():
        _row_gather_start(x_hbm, tok_ref, xbuf.at[0], sems.at[0], rows)

    @pl.when(i + 1 < nb)
    def _():
        _row_gather_start(x_hbm, tok_next_ref, xbuf.at[1 - slot], sems.at[1 - slot], rows)

    _row_gather_wait(x_hbm, xbuf.at[slot], sems.at[slot], rows)
    n_slab = xbuf.shape[2]
    x = jnp.concatenate([xbuf[slot, :, s, :] for s in range(n_slab)], axis=1).astype(BF16)
    a = jnp.dot(x, w1_ref[0], preferred_element_type=F32)
    b = jnp.dot(x, w3_ref[0], preferred_element_type=F32)
    h = (a * jax.nn.sigmoid(a) * b).astype(BF16)
    y = jnp.dot(h, w2_ref[0], preferred_element_type=F32)
    for s in range(n_slab):
        y_ref[:, s, :] = y[:, s * LANES:(s + 1) * LANES]


def _combine_kernel(s0_ref, s1_ref, s0n_ref, s1n_ref, yb_hbm, x2_ref, route_ref, gfin_ref, out_ref, ybuf, sems):
    i = pl.program_id(0)
    nb = pl.num_programs(0)
    rows = x2_ref.shape[0]
    slot = i % 2

    def start(a_ref, b_ref, s):
        _row_gather_start(yb_hbm, a_ref, ybuf.at[s, 0], sems.at[s, 0], rows)
        _row_gather_start(yb_hbm, b_ref, ybuf.at[s, 1], sems.at[s, 1], rows)

    @pl.when(i == 0)
    def _():
        start(s0_ref, s1_ref, 0)

    @pl.when(i + 1 < nb)
    def _():
        start(s0n_ref, s1n_ref, 1 - slot)

    _row_gather_wait(yb_hbm, ybuf.at[slot, 0], sems.at[slot, 0], rows)
    _row_gather_wait(yb_hbm, ybuf.at[slot, 1], sems.at[slot, 1], rows)
    route = route_ref[...]
    g0 = route[:, ROUTE_GATE0:ROUTE_GATE0 + 1]
    g1 = route[:, ROUTE_GATE1:ROUTE_GATE1 + 1]
    y = jnp.concatenate([g0 * ybuf[slot, 0, :, s, :] + g1 * ybuf[slot, 1, :, s, :] for s in range(ybuf.shape[3])],
                        axis=1)
    out_ref[...] = _rms(x2_ref[...] + y, gfin_ref[...])


def _moe_and_final(x2, u3, route, w_e1, w_e3, w_e2, g_final, tm, blk):
    n, d_model = x2.shape
    n_slab = d_model // LANES
    n_exp = w_e1.shape[0]
    a_total = n * EXPERT_TOPK
    n_blocks = -(-(a_total + n_exp * (blk - 1)) // blk)
    p_rows = n_blocks * blk

    fe = route[:, ROUTE_EXPERT0:ROUTE_EXPERT1 + 1].astype(I32).reshape(-1)
    onehot = (fe[:, None] == jnp.arange(n_exp, dtype=I32)[None, :]).astype(I32)
    csum = jnp.cumsum(onehot, axis=0)
    rank = jnp.sum((csum - onehot) * onehot, axis=1)
    counts = csum[-1]
    padded = (counts + blk - 1) // blk * blk
    pad_end = jnp.cumsum(padded)
    dest = (pad_end - padded)[fe] + rank
    ft = jnp.arange(a_total, dtype=I32) // EXPERT_TOPK
    slot_tok = jnp.zeros((p_rows,), I32).at[dest].set(ft)
    block_start = jnp.arange(n_blocks, dtype=I32) * blk
    block_exp = jnp.minimum(jnp.sum((pad_end[None, :] <= block_start[:, None]).astype(I32), axis=1), n_exp - 1)

    tok3 = slot_tok.reshape(n_blocks, 1, blk)
    smem_blk = lambda w, f: pl.BlockSpec((1, 1, w), f, memory_space=pltpu.SMEM)
    w_spec = lambda shp: pl.BlockSpec((1,) + shp[1:], lambda i, be: (be[i], 0, 0))
    yb = pl.pallas_call(
        _expert_kernel,
        grid_spec=pltpu.PrefetchScalarGridSpec(
            num_scalar_prefetch=1,
            grid=(n_blocks,),
            in_specs=[smem_blk(blk, lambda i, be: (i, 0, 0)),
                      smem_blk(blk, lambda i, be: (jnp.minimum(i + 1, n_blocks - 1), 0, 0)),
                      pl.BlockSpec(memory_space=pl.ANY),
                      w_spec(w_e1.shape), w_spec(w_e3.shape), w_spec(w_e2.shape)],
            out_specs=pl.BlockSpec((blk, n_slab, LANES), lambda i, be: (i, 0, 0)),
            scratch_shapes=[pltpu.VMEM((2, blk, n_slab, LANES), F32), pltpu.SemaphoreType.DMA((2,))],
        ),
        out_shape=jax.ShapeDtypeStruct((p_rows, n_slab, LANES), F32),
        compiler_params=_cparams(("arbitrary",)),
        name="moe_experts",
    )(block_exp, tok3, tok3, u3, w_e1.astype(BF16), w_e3.astype(BF16), w_e2.astype(BF16))

    tm = min(tm, n)
    assert n % tm == 0
    nt = n // tm
    d2 = dest.reshape(n, EXPERT_TOPK)
    s0 = d2[:, 0].reshape(nt, 1, tm)
    s1 = d2[:, 1].reshape(nt, 1, tm)
    cur = lambda i: (i, 0, 0)
    nxt = lambda i: (jnp.minimum(i + 1, nt - 1), 0, 0)
    smem_blk2 = lambda f: pl.BlockSpec((1, 1, tm), f, memory_space=pltpu.SMEM)
    row_spec = lambda w: pl.BlockSpec((tm, w), lambda i: (i, 0))
    return pl.pallas_call(
        _combine_kernel,
        grid=(nt,),
        in_specs=[smem_blk2(cur), smem_blk2(cur), smem_blk2(nxt), smem_blk2(nxt),
                  pl.BlockSpec(memory_space=pl.ANY), row_spec(d_model), row_spec(LANES), _const_spec((1, d_model))],
        out_specs=row_spec(d_model),
        out_shape=jax.ShapeDtypeStruct((n, d_model), F32),
        scratch_shapes=[pltpu.VMEM((2, EXPERT_TOPK, tm, n_slab, LANES), F32),
                        pltpu.SemaphoreType.DMA((2, EXPERT_TOPK))],
        compiler_params=_cparams(("arbitrary",)),
        name="moe_combine",
    )(s0, s1, s0, s1, yb, x2, route, g_final.reshape(1, d_model))


def _layer(x, pos, conv_buf, h0, attend, mk, mv, lw, g_final, tiles):
    batch, t_len, d_model = x.shape
    n = batch * t_len
    d_lru = lw['w_conv'].shape[1]
    d_att = N_HEADS * HEAD_DIM
    d_idx = IDX_HEADS * IDX_DIM
    x2d = x.reshape(n, d_model)
    xl, q, k, v, qi, kiwi, gl, ga = _in_proj(x2d, lw['g_mix'], lw['w_in_packed'], pos, t_len, d_lru, d_att, d_idx,
                                             tiles['in_proj'])
    att = attend(qi, q, kiwi, k, v)
    hs, h_last, new_buf = _rglru(xl.reshape(batch, t_len, d_lru), conv_buf, h0, lw['w_conv'], lw['b_conv'],
                                 lw['w_ra'], lw['b_ra'], lw['w_ri'], lw['b_ri'], lw['lru_lambda'],
                                 tiles['lru_t'], tiles['lru_b'])
    x1, qc = _mix(x2d, hs.reshape(n, d_lru), att, gl, ga, lw['w_lru_out'], lw['w_att_out'], lw['w_mix_out'],
                  lw['g_cross'], lw['w_cq'], tiles['mix'])
    x2, u3, route = _cross(x1, qc, mk, mv, lw['w_co'], lw['g_ffn'], lw['w_router'], lw['b_router'], batch, t_len,
                           tiles['cross'], tiles['cross_seqs'])
    y = _moe_and_final(x2, u3, route, lw['w_e1'], lw['w_e3'], lw['w_e2'], g_final, tiles['combine'],
                       tiles['moe_rows'])
    k5 = k.reshape(batch, t_len, N_HEADS, HEAD_DIM)
    v5 = v.reshape(batch, t_len, N_HEADS, HEAD_DIM)
    ki = kiwi[:, :IDX_DIM].reshape(batch, t_len, IDX_DIM)
    return y.reshape(batch, t_len, d_model), new_buf, h_last, k5, v5, ki


def kernel(x_prompt, mem_prompt, x_sample, cache_k, cache_v, cache_kidx, cache_mem_k, cache_mem_v, state_conv, state_lru, page_table, g_mix, w_in, w_conv, b_conv, w_ra, b_ra, w_ri, b_ri, lru_lambda, w_lru_out, w_att_out, w_mix_out, g_cross, g_mem, w_cq, w_mk, w_mv, w_co, g_ffn, w_rg, b_rg, w_re, b_re, w_e1, w_e3, w_e2, g_final):
    depth = w_in.shape[0]
    assert depth == 1, "the final norm is fused into the last layer's MoE combine; one layer supported"
    l = 0
    b_p, t_p, _ = x_prompt.shape
    b_s, t_s, _ = x_sample.shape
    past = page_table.shape[1] * cache_k.shape[2]
    d_lru = w_conv.shape[2]
    d_att = N_HEADS * HEAD_DIM
    d_idx = IDX_HEADS * IDX_DIM
    w_router, b_router = _pack_router(w_rg[l], b_rg[l], w_re[l], b_re[l])
    lw = {
        'g_mix': g_mix[l], 'w_in_packed': _pack_w_in(w_in[l], d_lru, d_att, d_idx),
        'w_conv': w_conv[l], 'b_conv': b_conv[l], 'w_ra': w_ra[l], 'b_ra': b_ra[l], 'w_ri': w_ri[l], 'b_ri': b_ri[l],
        'lru_lambda': lru_lambda[l], 'w_lru_out': w_lru_out[l], 'w_att_out': w_att_out[l], 'w_mix_out': w_mix_out[l],
        'g_cross': g_cross[l], 'w_cq': w_cq[l], 'w_co': w_co[l], 'g_ffn': g_ffn[l],
        'w_router': w_router, 'b_router': b_router, 'w_e1': w_e1[l], 'w_e3': w_e3[l], 'w_e2': w_e2[l],
    }

    mk_p, mv_p, mkb_p, mvb_p = _mem_kv(mem_prompt, g_mem[l], w_mk[l], w_mv[l])
    attend_p = functools.partial(_prompt_attention, batch=b_p, t_len=t_p)
    tiles_p = {'in_proj': 512, 'lru_t': 128, 'lru_b': 8, 'mix': 512, 'cross': 512, 'cross_seqs': 1,
               'combine': 256, 'moe_rows': 256}
    y_p, buf_p, h_p, k_p, v_p, ki_p = _layer(
        x_prompt, jnp.arange(t_p), jnp.zeros((b_p, CONV_W - 1, d_lru), F32), jnp.zeros((b_p, d_lru), F32),
        attend_p, mkb_p, mvb_p, lw, g_final, tiles_p)

    def attend_s(qi, q, kiwi, k, v):
        return _sample_attention(qi, q, kiwi, k, v, cache_k, cache_v, cache_kidx, page_table, l, t_s,
                                 pg_sel=32, pg_att=16)
    tiles_s = {'in_proj': 512, 'lru_t': 8, 'lru_b': 8, 'mix': 512, 'cross': t_s, 'cross_seqs': 16,
               'combine': 256, 'moe_rows': 128}
    flat_mem = lambda m: m.reshape(m.shape[0], m.shape[1], -1).astype(BF16)
    y_s, buf_s, h_s, k_s, v_s, ki_s = _layer(
        x_sample, past + jnp.arange(t_s), state_conv[l], state_lru[l], attend_s,
        flat_mem(cache_mem_k[l]), flat_mem(cache_mem_v[l]), lw, g_final, tiles_s)

    st = lambda a: a[None]
    return (y_p, y_s, st(k_p), st(v_p), st(ki_p), st(buf_p), st(h_p), st(mk_p), st(mv_p),
            st(k_s), st(v_s), st(ki_s), st(buf_s), st(h_s))
```

```python
import functools
import math

import jax
import jax.numpy as jnp
import numpy as np
from jax import lax
from jax.experimental import pallas as pl
from jax.experimental.pallas import tpu as pltpu

F32 = jnp.float32
BF16 = jnp.bfloat16
I32 = jnp.int32

N_HEADS = 8
HEAD_DIM = 64
IDX_HEADS = 8
IDX_DIM = 64
TOPK_MAX = 256
Q_BLOCK = 128
ROPE_THETA = 10000.0
CONV_W = 4
LRU_C = 8.0
LRU_BLOCKS = 8
C_HEADS = 4
N_GROUPS = 4
EXPERTS_PER_GROUP = 8
EXPERT_TOPK = 2
MOE_BLOCK = 128
NORM_EPS = 1e-6

LANES = 128
VMEM_LIMIT = 56 * 1024 * 1024


def _cparams(sem):
    return pltpu.CompilerParams(dimension_semantics=sem, vmem_limit_bytes=VMEM_LIMIT)


def _const_spec(shape):
    nd = len(shape)
    return pl.BlockSpec(shape, lambda *_: (0,) * nd)


def _rms(x, g):
    return x * lax.rsqrt(jnp.mean(x * x, axis=-1, keepdims=True) + NORM_EPS) * g


def _rope_rot(x, cos, sin_signed):
    w = x.shape[1]
    reps = w // LANES
    fwd = pltpu.roll(x, 32, axis=1)
    bwd = pltpu.roll(x, w - 32, axis=1)
    lane = lax.broadcasted_iota(I32, x.shape, 1)
    rot = jnp.where((lane % 64) < 32, bwd, fwd)
    if reps > 1:
        cos = jnp.concatenate([cos] * reps, axis=1)
        sin_signed = jnp.concatenate([sin_signed] * reps, axis=1)
    return x * cos + rot * sin_signed


def _in_proj_kernel(x_ref, g_ref, w_ref, cos_ref, sin_ref,
                    xl_ref, q_ref, k_ref, v_ref, qi_ref, kiwi_ref, gl_ref, ga_ref, *, d_lru, d_att, d_idx, d_model):
    u = _rms(x_ref[...], g_ref[...]).astype(BF16)
    cos = cos_ref[...]
    sin = sin_ref[...]

    def proj(c0, width):
        return jnp.dot(u, w_ref[:, c0:c0 + width], preferred_element_type=F32)

    c = 0
    xl_ref[...] = proj(c, d_lru); c += d_lru
    q_ref[...] = (_rope_rot(proj(c, d_att), cos, sin) * (HEAD_DIM ** -0.5)).astype(BF16); c += d_att
    k_ref[...] = _rope_rot(proj(c, d_att), cos, sin); c += d_att
    v_ref[...] = proj(c, d_att); c += d_att
    qi_ref[...] = (_rope_rot(proj(c, d_idx), cos, sin) * (IDX_DIM ** -0.5)).astype(BF16); c += d_idx
    kw = proj(c, LANES); c += LANES
    lane = lax.broadcasted_iota(I32, kw.shape, 1)
    kiwi_ref[...] = jnp.where(lane < IDX_DIM, _rope_rot(kw, cos, sin), kw * (IDX_HEADS ** -0.5))
    gl_ref[...] = jax.nn.sigmoid(proj(c, d_model)); c += d_model
    ga_ref[...] = jax.nn.sigmoid(proj(c, d_model))


def _rope_tables(pos, rows):
    inv = ROPE_THETA ** (-jnp.arange(0, HEAD_DIM, 2, dtype=F32) / HEAD_DIM)
    ang = pos.astype(F32)[:, None] * inv[None, :]
    cos = jnp.cos(ang)
    sin = jnp.sin(ang)
    cos_t = jnp.concatenate([cos, cos, cos, cos], axis=1)
    sin_t = jnp.concatenate([-sin, sin, -sin, sin], axis=1)
    reps = rows // pos.shape[0]
    if reps > 1:
        cos_t = jnp.tile(cos_t, (reps, 1))
        sin_t = jnp.tile(sin_t, (reps, 1))
    return cos_t, sin_t


def _in_proj(x2d, g, w_packed, pos, seq_len, d_lru, d_att, d_idx, tm):
    n, d_model = x2d.shape
    tm = min(tm, n)
    assert n % tm == 0
    rows = max(seq_len, tm)
    assert rows % tm == 0 and (tm % seq_len == 0 or seq_len % tm == 0)
    cos_t, sin_t = _rope_tables(pos, rows)
    nt = rows // tm
    row_spec = lambda w: pl.BlockSpec((tm, w), lambda i: (i, 0))
    tab_spec = pl.BlockSpec((tm, LANES), lambda i: (i % nt, 0))
    kern = functools.partial(_in_proj_kernel, d_lru=d_lru, d_att=d_att, d_idx=d_idx, d_model=d_model)
    out_shape = (
        jax.ShapeDtypeStruct((n, d_lru), F32),
        jax.ShapeDtypeStruct((n, d_att), BF16),
        jax.ShapeDtypeStruct((n, d_att), F32),
        jax.ShapeDtypeStruct((n, d_att), F32),
        jax.ShapeDtypeStruct((n, d_idx), BF16),
        jax.ShapeDtypeStruct((n, LANES), F32),
        jax.ShapeDtypeStruct((n, d_model), F32),
        jax.ShapeDtypeStruct((n, d_model), F32),
    )
    return pl.pallas_call(
        kern,
        grid=(n // tm,),
        in_specs=[row_spec(d_model), _const_spec((1, d_model)), _const_spec(w_packed.shape), tab_spec, tab_spec],
        out_specs=tuple(row_spec(s.shape[1]) for s in out_shape),
        out_shape=out_shape,
        compiler_params=_cparams(("parallel",)),
        name="in_proj",
    )(x2d, g.reshape(1, d_model), w_packed, cos_t, sin_t)


def _pack_w_in(w_in, d_lru, d_att, d_idx):
    d_model = w_in.shape[0]
    c = d_lru + 3 * d_att + d_idx
    kiwi = w_in[:, c:c + IDX_DIM + IDX_HEADS]
    pad = jnp.zeros((d_model, LANES - IDX_DIM - IDX_HEADS), w_in.dtype)
    return jnp.concatenate([w_in[:, :c], kiwi, pad, w_in[:, c + IDX_DIM + IDX_HEADS:]], axis=1).astype(BF16)


CONV_PAD = 8


def _softplus(x):
    return jnp.maximum(x, 0.0) + jnp.log1p(jnp.exp(-jnp.abs(x)))


def _rglru_kernel(xl_ref, buf0_ref, h0_ref, wconv_ref, bconv_ref, wra_ref, bra_ref, wri_ref, bri_ref, lam_ref,
                  hs_ref, hlast_ref, newbuf_ref, xpad_scr, a_scr, u_scr, h_scr, *, tc, bb):
    j = pl.program_id(1)
    d = xl_ref.shape[-1]
    hist = CONV_W - 1

    @pl.when(j == 0)
    def _():
        xpad_scr[:, 0:CONV_PAD - hist, :] = jnp.zeros((bb, CONV_PAD - hist, d), F32)
        xpad_scr[:, CONV_PAD - hist:CONV_PAD, :] = buf0_ref[...]
        h_scr[...] = h0_ref[...]

    xpad_scr[:, CONV_PAD:CONV_PAD + tc, :] = xl_ref[...]
    xc = jnp.zeros((bb, tc, d), F32) + bconv_ref[...]
    for i in range(CONV_W):
        off = CONV_PAD - hist + i
        xc = xc + xpad_scr[:, off:off + tc, :] * wconv_ref[i:i + 1, :]
    newbuf_ref[...] = xpad_scr[:, CONV_PAD + tc - hist:CONV_PAD + tc, :]
    xpad_scr[:, 0:CONV_PAD, :] = xpad_scr[:, tc:tc + CONV_PAD, :]

    xc2 = xc.reshape(bb * tc, d)
    xb = xc2.astype(BF16)
    r = jax.nn.sigmoid(jnp.dot(xb, wra_ref[...], preferred_element_type=F32) + bra_ref[...])
    g = jax.nn.sigmoid(jnp.dot(xb, wri_ref[...], preferred_element_type=F32) + bri_ref[...])
    log_a = (-LRU_C) * r * _softplus(-lam_ref[...])
    a = jnp.exp(log_a)
    mult = jnp.sqrt(-jnp.tanh(log_a) * (a * a + 1.0))
    a_scr[...] = a.reshape(bb, tc, d)
    u_scr[...] = (mult * (g * xc2)).reshape(bb, tc, d)

    def step(t, hs):
        new = []
        for b in range(bb):
            h = a_scr[b, pl.ds(t, 1), :] * hs[b] + u_scr[b, pl.ds(t, 1), :]
            u_scr[b, pl.ds(t, 1), :] = h
            new.append(h)
        return tuple(new)

    h_fin = lax.fori_loop(0, tc, step, tuple(h_scr[b] for b in range(bb)))
    for b in range(bb):
        h_scr[b] = h_fin[b]
        hlast_ref[b] = h_fin[b]
    hs_ref[...] = u_scr[...].astype(hs_ref.dtype)


def _block_diag(w):
    nb, c, _ = w.shape
    eye = jnp.eye(nb, dtype=w.dtype)
    return (eye[:, None, :, None] * w[:, :, None, :]).reshape(nb * c, nb * c)


def _rglru(xl, buf0, h0, w_conv, b_conv, w_ra, b_ra, w_ri, b_ri, lam, tc, bb):
    b, t, d = xl.shape
    tc = min(tc, t)
    bb = min(bb, b)
    assert t % tc == 0 and b % bb == 0 and tc % 8 == 0
    row = lambda a: a.reshape(1, d)
    kern = functools.partial(_rglru_kernel, tc=tc, bb=bb)
    hist = CONV_W - 1
    out_shape = (
        jax.ShapeDtypeStruct((b, t, d), BF16 if tc % 16 == 0 else F32),
        jax.ShapeDtypeStruct((b, 1, d), F32),
        jax.ShapeDtypeStruct((b, hist, d), F32),
    )
    hs, h_last, new_buf = pl.pallas_call(
        kern,
        grid=(b // bb, t // tc),
        in_specs=[
            pl.BlockSpec((bb, tc, d), lambda i, j: (i, j, 0)),
            pl.BlockSpec((bb, hist, d), lambda i, j: (i, 0, 0)),
            pl.BlockSpec((bb, 1, d), lambda i, j: (i, 0, 0)),
            _const_spec((CONV_W, d)), _const_spec((1, d)),
            _const_spec((d, d)), _const_spec((1, d)), _const_spec((d, d)), _const_spec((1, d)), _const_spec((1, d)),
        ],
        out_specs=(
            pl.BlockSpec((bb, tc, d), lambda i, j: (i, j, 0)),
            pl.BlockSpec((bb, 1, d), lambda i, j: (i, 0, 0)),
            pl.BlockSpec((bb, hist, d), lambda i, j: (i, 0, 0)),
        ),
        out_shape=out_shape,
        scratch_shapes=[
            pltpu.VMEM((bb, tc + CONV_PAD, d), F32),
            pltpu.VMEM((bb, tc, d), F32),
            pltpu.VMEM((bb, tc, d), F32),
            pltpu.VMEM((bb, 1, d), F32),
        ],
        compiler_params=_cparams(("parallel", "arbitrary")),
        name="rglru",
    )(xl, buf0, h0.reshape(b, 1, d), w_conv, row(b_conv),
      _block_diag(w_ra).astype(BF16), row(b_ra), _block_diag(w_ri).astype(BF16), row(b_ri), row(lam))
    return hs, h_last.reshape(b, d), new_buf


INT_MIN = -2 ** 31
NEG_INF_KEY = -2139095041
KEY_GROUP = 4
SELECT_SEQS = 8


def _sortable_key(score):
    score = jnp.where(score == 0.0, 0.0, score)
    bits = pltpu.bitcast(score, I32)
    return bits ^ ((bits >> 31) & 0x7FFFFFFF)


def _pair_block_diag(x):
    lane = lax.broadcasted_iota(I32, x.shape, 1)
    zero = jnp.zeros_like(x)
    return jnp.concatenate([jnp.where(lane < 64, x, zero), jnp.where(lane >= 64, x, zero)], axis=0)


def _dot_nt(a, b):
    return lax.dot_general(a, b, (((1,), (1,)), ((), ())), preferred_element_type=F32)


def _selection_bias(kk, eq, tie_rank, theta, need):
    take = jnp.where(eq, tie_rank, jnp.inf) <= need
    b = jnp.where(take, 0.0, -jnp.inf)
    b = jnp.where(kk > theta, 0.0, b)
    return jnp.where(kk == NEG_INF_KEY, -jnp.inf, b)


def _prompt_attn_kernel(qi_ref, q_ref, kiwi_ref, k_ref, v_ref, att_ref,
                        kdup_scr, kbf_scr, vT_scr, qibd_scr, qbd_scr, key_scr, bias_scr, logit_scr, oT_scr, acc_scr,
                        *, topk, qb):
    j = pl.program_id(1)
    t_len = k_ref.shape[0]
    n_pairs = q_ref.shape[1] // LANES
    nk = j + 1

    def rows_of(c):
        return pl.ds(pl.multiple_of(c * qb, qb), qb)

    @pl.when(j == 0)
    def _():
        def prep(c, carry):
            rows = rows_of(c)
            kw = kiwi_ref[rows, :]
            lane = lax.broadcasted_iota(I32, kw.shape, 1)
            kdup_scr[rows, :] = jnp.where(lane < IDX_DIM, kw, pltpu.roll(kw, IDX_DIM, axis=1)).astype(BF16)
            kbf_scr[rows, :] = k_ref[rows, :].astype(BF16)
            vT_scr[:, rows] = v_ref[rows, :].T.astype(BF16)
            return carry
        lax.fori_loop(0, t_len // qb, prep, 0)

    for p in range(n_pairs):
        qibd_scr[p] = _pair_block_diag(qi_ref[:, p * LANES:(p + 1) * LANES])
        qbd_scr[p] = _pair_block_diag(q_ref[:, p * LANES:(p + 1) * LANES])
    w_t = kiwi_ref[rows_of(j), :].T

    def key_loop(body, init):
        wide = KEY_GROUP * qb
        n_groups = (nk + KEY_GROUP - 1) // KEY_GROUP
        return lax.fori_loop(0, n_groups, lambda i, c: body(pl.multiple_of(i * wide, wide), wide, c), init)

    def idx_body(r0, nr, carry):
        rows = pl.ds(r0, nr)
        kd = kdup_scr[rows, :]
        sc = jnp.zeros((nr, qb), F32)
        for p in range(n_pairs):
            s2 = jnp.maximum(_dot_nt(kd, qibd_scr[p]), 0.0)
            h = IDX_DIM + 2 * p
            sc = sc + w_t[h:h + 1, :] * s2[:, :qb] + w_t[h + 1:h + 2, :] * s2[:, qb:]
        kpos = r0 + lax.broadcasted_iota(I32, (nr, qb), 0)
        qpos = j * qb + lax.broadcasted_iota(I32, (nr, qb), 1)
        key_scr[rows, :] = _sortable_key(jnp.where(kpos <= qpos, sc, -jnp.inf))
        return carry
    key_loop(idx_body, 0)

    def count(pred):
        def body(r0, nr, acc):
            hit = jnp.where(pred(key_scr[pl.ds(r0, nr), :]), 1, 0).astype(I32)
            return acc + jnp.sum(hit.reshape(nr // 8, 8, qb), axis=0)
        return jnp.sum(key_loop(body, jnp.zeros((8, qb), I32)), axis=0, keepdims=True)

    def bit_step(i, prefix):
        cand_u = prefix | jnp.left_shift(jnp.int32(1), 31 - i)
        cand_s = cand_u ^ INT_MIN
        return jnp.where(count(lambda kk: kk >= cand_s) >= topk, cand_u, prefix)
    theta = lax.fori_loop(0, 32, bit_step, jnp.zeros((1, qb), I32)) ^ INT_MIN

    need = (topk - count(lambda kk: kk > theta)).astype(F32)
    tri = jnp.where(lax.broadcasted_iota(I32, (qb, qb), 0) >= lax.broadcasted_iota(I32, (qb, qb), 1),
                    1.0, 0.0).astype(BF16)

    def mask_body(r0, nr, run):
        for s in range(nr // qb):
            rows = pl.ds(r0 + s * qb, qb)
            kk = key_scr[rows, :]
            eq = kk == theta
            pre = jnp.dot(tri, jnp.where(eq, 1.0, 0.0).astype(BF16), preferred_element_type=F32)
            bias_scr[rows, :] = _selection_bias(kk, eq, run + pre, theta, need)
            run = run + pre[qb - 1:qb, :]
        return run
    key_loop(mask_body, jnp.zeros((1, qb), F32))

    def pair_lanes(p):
        return slice(p * LANES, (p + 1) * LANES)

    def pass1(r0, nr, m8s):
        rows = pl.ds(r0, nr)
        bias = bias_scr[rows, :]
        bias2 = jnp.concatenate([bias, bias], axis=1)
        new = []
        for p in range(n_pairs):
            lg = _dot_nt(kbf_scr[rows, pair_lanes(p)], qbd_scr[p]) + bias2
            logit_scr[p, rows, :] = lg
            new.append(jnp.maximum(m8s[p], jnp.max(lg.reshape(nr // 8, 8, 2 * qb), axis=0)))
        return tuple(new)
    m8s = key_loop(pass1, tuple(jnp.full((8, 2 * qb), -jnp.inf, F32) for _ in range(n_pairs)))
    ms = [jnp.max(m8, axis=0, keepdims=True) for m8 in m8s]

    acc_scr[...] = jnp.zeros(acc_scr.shape, F32)

    def pass2(r0, nr, l8s):
        rows = pl.ds(r0, nr)
        new = []
        for p in range(n_pairs):
            pe = jnp.exp(logit_scr[p, rows, :] - ms[p])
            new.append(l8s[p] + jnp.sum(pe.reshape(nr // 8, 8, 2 * qb), axis=0))
            acc_scr[p] += jnp.dot(vT_scr[pair_lanes(p), rows], pe.astype(BF16), preferred_element_type=F32)
        return tuple(new)
    l8s = key_loop(pass2, tuple(jnp.zeros((8, 2 * qb), F32) for _ in range(n_pairs)))

    for p in range(n_pairs):
        o = acc_scr[p] / jnp.sum(l8s[p], axis=0, keepdims=True)
        oT_scr[p * LANES:p * LANES + HEAD_DIM, :] = o[:HEAD_DIM, :qb]
        oT_scr[p * LANES + HEAD_DIM:(p + 1) * LANES, :] = o[HEAD_DIM:, qb:]
    att_ref[...] = oT_scr[...].T.astype(att_ref.dtype)


def _prompt_attention(qi, q, kiwi, k, v, batch, t_len):
    n, d_att = q.shape
    qb = min(Q_BLOCK, t_len)
    assert qb == LANES and t_len % (KEY_GROUP * qb) == 0
    nqb = t_len // qb
    topk = min(TOPK_MAX, t_len // 4)
    n_pairs = d_att // LANES
    kern = functools.partial(_prompt_attn_kernel, topk=topk, qb=qb)
    blk_spec = lambda w: pl.BlockSpec((qb, w), lambda b, j: (b * nqb + j, 0))
    seq_spec = lambda w: pl.BlockSpec((t_len, w), lambda b, j: (b, 0))
    return pl.pallas_call(
        kern,
        grid=(batch, nqb),
        in_specs=[blk_spec(qi.shape[1]), blk_spec(d_att), seq_spec(LANES), seq_spec(d_att), seq_spec(d_att)],
        out_specs=blk_spec(d_att),
        out_shape=jax.ShapeDtypeStruct((n, d_att), BF16),
        scratch_shapes=[
            pltpu.VMEM((t_len, LANES), BF16),
            pltpu.VMEM((t_len, d_att), BF16),
            pltpu.VMEM((d_att, t_len), BF16),
            pltpu.VMEM((n_pairs, 2 * qb, LANES), BF16),
            pltpu.VMEM((n_pairs, 2 * qb, LANES), BF16),
            pltpu.VMEM((t_len, qb), I32),
            pltpu.VMEM((t_len, qb), F32),
            pltpu.VMEM((n_pairs, t_len, 2 * qb), F32),
            pltpu.VMEM((d_att, qb), F32),
            pltpu.VMEM((n_pairs, LANES, 2 * qb), F32),
        ],
        compiler_params=_cparams(("parallel", "arbitrary")),
        name="prompt_attn",
    )(qi, q, kiwi, k, v)


def _sample_score_kernel(pt_ref, qi_ref, wi_ref, kinew_ref, *rest, pg, past, page, t_new):
    del pt_ref
    page_refs = rest[:pg]
    key_ref = rest[pg]
    g = pl.program_id(1)
    qi = qi_ref[...]
    wi = wi_ref[...]

    def head_sum(s):
        s = jnp.maximum(s, 0.0) * wi
        return jnp.sum(s.reshape(IDX_HEADS, t_new, s.shape[1]), axis=0)

    k_t = jnp.concatenate([r[0, 0] for r in page_refs], axis=1).astype(BF16)
    off = pl.multiple_of(g * (pg * page), page)
    key_ref[0, :, pl.ds(off, pg * page)] = _sortable_key(head_sum(jnp.dot(qi, k_t, preferred_element_type=F32)))

    @pl.when(g == pl.num_programs(1) - 1)
    def _():
        knew = jnp.concatenate([kinew_ref[:, :IDX_DIM], jnp.zeros((page - t_new, IDX_DIM), F32)], axis=0)
        sc = head_sum(_dot_nt(qi, knew.astype(BF16)))
        tok = lax.broadcasted_iota(I32, sc.shape, 0)
        kk_i = lax.broadcasted_iota(I32, sc.shape, 1)
        key_ref[0, :, past:past + page] = _sortable_key(jnp.where(kk_i <= tok, sc, -jnp.inf))


def _sample_threshold_kernel(key_ref, bias_ref, *, topk, page):
    n_seq, t_new, l_pad = key_ref.shape
    rows = n_seq * t_new
    n_chunks = l_pad // page

    def chunk(c):
        return key_ref[:, :, c * page:(c + 1) * page].reshape(rows, page)

    def count(pred):
        acc = jnp.zeros((rows, page), I32)
        for c in range(n_chunks):
            acc = acc + jnp.where(pred(chunk(c)), 1, 0).astype(I32)
        return jnp.sum(acc, axis=1, keepdims=True)

    def bit_step(i, prefix):
        cand_u = prefix | jnp.left_shift(jnp.int32(1), 31 - i)
        cand_s = cand_u ^ INT_MIN
        return jnp.where(count(lambda kk: kk >= cand_s) >= topk, cand_u, prefix)
    theta = lax.fori_loop(0, 32, bit_step, jnp.zeros((rows, 1), I32)) ^ INT_MIN
    need = (topk - count(lambda kk: kk > theta)).astype(F32)

    row_i = lax.broadcasted_iota(I32, (page, page), 0)
    col_i = lax.broadcasted_iota(I32, (page, page), 1)
    triu = jnp.where(row_i <= col_i, 1.0, 0.0).astype(BF16)
    run = jnp.zeros((rows, 1), F32)
    for c in range(n_chunks):
        kk = chunk(c)
        eq = kk == theta
        pre = jnp.dot(jnp.where(eq, 1.0, 0.0).astype(BF16), triu, preferred_element_type=F32)
        bias = _selection_bias(kk, eq, run + pre, theta, need)
        bias_ref[:, :, c * page:(c + 1) * page] = bias.reshape(n_seq, t_new, page)
        run = run + pre[:, page - 1:page]


def _sample_attn_kernel(pt_ref, q_ref, bias_ref, knew_ref, vnew_ref, *rest, pg, past, page, t_new):
    del pt_ref
    k_refs = rest[:pg]
    v_refs = rest[pg:2 * pg]
    out_ref = rest[2 * pg]
    m_scr, l_scr, acc_scr = rest[2 * pg + 1:]
    g = pl.program_id(1)
    qbd = q_ref[...]

    @pl.when(g == 0)
    def _():
        m_scr[...] = jnp.full(m_scr.shape, -jnp.inf, F32)
        l_scr[...] = jnp.zeros(l_scr.shape, F32)
        acc_scr[...] = jnp.zeros(acc_scr.shape, F32)

    def attend(lg, bias, pv_fn):
        lg = lg + jnp.concatenate([bias] * N_HEADS, axis=0)
        m_old = m_scr[...]
        m_new = jnp.maximum(m_old, jnp.max(lg, axis=1, keepdims=True))
        m_safe = jnp.where(m_new == -jnp.inf, 0.0, m_new)
        alpha = jnp.exp(m_old - m_safe)
        pe = jnp.exp(lg - m_safe)
        l_scr[...] = alpha * l_scr[...] + jnp.sum(pe, axis=1, keepdims=True)
        acc_scr[...] = alpha * acc_scr[...] + pv_fn(pe.astype(BF16))
        m_scr[...] = m_new

    k_t = jnp.concatenate([r[0, 0] for r in k_refs], axis=1).astype(BF16)
    v_t = jnp.concatenate([r[0, 0] for r in v_refs], axis=1).astype(BF16)
    off = pl.multiple_of(g * (pg * page), page)
    attend(jnp.dot(qbd, k_t, preferred_element_type=F32), bias_ref[0, :, pl.ds(off, pg * page)],
           lambda pb: _dot_nt(pb, v_t))

    @pl.when(g == pl.num_programs(1) - 1)
    def _():
        pad = jnp.zeros((page - t_new, knew_ref.shape[1]), F32)
        k_new = jnp.concatenate([knew_ref[...], pad], axis=0).astype(BF16)
        v_new = jnp.concatenate([vnew_ref[...], pad], axis=0).astype(BF16)
        attend(_dot_nt(qbd, k_new), bias_ref[0, :, past:past + page],
               lambda pb: jnp.dot(pb, v_new, preferred_element_type=F32))
        out_ref[0] = acc_scr[...] / l_scr[...]


def _to_head_major(x, batch, t_new, heads, dim):
    return x.reshape(batch, t_new, heads, dim).transpose(0, 2, 1, 3).reshape(batch * heads * t_new, dim)


def _sample_attention(qi, q, kiwi, k_new, v_new, cache_k, cache_v, cache_kidx, page_table, layer, t_new, pg_sel, pg_att):
    batch, n_pages = page_table.shape
    depth, n_pool, page = cache_k.shape[:3]
    past = n_pages * page
    pg_sel = min(pg_sel, n_pages)
    pg_att = min(pg_att, n_pages)
    assert n_pages % pg_sel == 0 and n_pages % pg_att == 0 and page == LANES and t_new == 8
    l_pad = past + page
    topk = min(TOPK_MAX, (past + t_new) // 4)
    rows = IDX_HEADS * t_new
    d_att = N_HEADS * HEAD_DIM

    kidx_t = cache_kidx.transpose(0, 1, 3, 2)
    k_t = cache_k.transpose(0, 1, 3, 4, 2).reshape(depth, n_pool, d_att, page)
    v_t = cache_v.transpose(0, 1, 3, 4, 2).reshape(depth, n_pool, d_att, page)

    qi_hm = _to_head_major(qi, batch, t_new, IDX_HEADS, IDX_DIM)
    wi_col = kiwi[:, IDX_DIM:IDX_DIM + IDX_HEADS].reshape(batch, t_new, IDX_HEADS).transpose(0, 2, 1).reshape(-1, 1)
    seq_spec = lambda r, w: pl.BlockSpec((r, w), lambda b, g, pt: (b, 0))
    bias_spec = pl.BlockSpec((1, t_new, l_pad), lambda b, g, pt: (b, 0, 0))

    def page_spec(arr, i, pg):
        return pl.BlockSpec((1, 1) + arr.shape[2:], lambda b, g, pt: (layer, pt[b, g * pg + i], 0, 0))

    keys = pl.pallas_call(
        functools.partial(_sample_score_kernel, pg=pg_sel, past=past, page=page, t_new=t_new),
        grid_spec=pltpu.PrefetchScalarGridSpec(
            num_scalar_prefetch=1,
            grid=(batch, n_pages // pg_sel),
            in_specs=[seq_spec(rows, IDX_DIM), seq_spec(rows, 1), seq_spec(t_new, LANES)]
                     + [page_spec(kidx_t, i, pg_sel) for i in range(pg_sel)],
            out_specs=bias_spec,
        ),
        out_shape=jax.ShapeDtypeStruct((batch, t_new, l_pad), I32),
        compiler_params=_cparams(("parallel", "arbitrary")),
        name="sample_score",
    )(page_table, qi_hm, wi_col, kiwi, *([kidx_t] * pg_sel))

    sel_seqs = min(SELECT_SEQS, batch)
    assert batch % sel_seqs == 0
    sel_spec = pl.BlockSpec((sel_seqs, t_new, l_pad), lambda b: (b, 0, 0))
    bias = pl.pallas_call(
        functools.partial(_sample_threshold_kernel, topk=topk, page=page),
        grid=(batch // sel_seqs,),
        in_specs=[sel_spec],
        out_specs=sel_spec,
        out_shape=jax.ShapeDtypeStruct((batch, t_new, l_pad), F32),
        compiler_params=_cparams(("parallel",)),
        name="sample_select",
    )(keys)

    q4 = q.reshape(batch, t_new, N_HEADS, HEAD_DIM)
    eye = jnp.eye(N_HEADS, dtype=q.dtype)
    qbd = (q4.transpose(0, 2, 1, 3)[:, :, :, None, :] * eye[None, :, None, :, None]).reshape(batch * rows, d_att)
    out = pl.pallas_call(
        functools.partial(_sample_attn_kernel, pg=pg_att, past=past, page=page, t_new=t_new),
        grid_spec=pltpu.PrefetchScalarGridSpec(
            num_scalar_prefetch=1,
            grid=(batch, n_pages // pg_att),
            in_specs=[seq_spec(rows, d_att), bias_spec, seq_spec(t_new, d_att), seq_spec(t_new, d_att)]
                     + [page_spec(k_t, i, pg_att) for i in range(pg_att)]
                     + [page_spec(v_t, i, pg_att) for i in range(pg_att)],
            out_specs=pl.BlockSpec((1, rows, d_att), lambda b, g, pt: (b, 0, 0)),
            scratch_shapes=[pltpu.VMEM((rows, 1), F32), pltpu.VMEM((rows, 1), F32), pltpu.VMEM((rows, d_att), F32)],
        ),
        out_shape=jax.ShapeDtypeStruct((batch, rows, d_att), F32),
        compiler_params=_cparams(("parallel", "arbitrary")),
        name="sample_attn",
    )(page_table, qbd, bias, k_new, v_new, *([k_t] * pg_att), *([v_t] * pg_att))
    o5 = out.reshape(batch, N_HEADS, t_new, N_HEADS, HEAD_DIM)
    att = jnp.stack([o5[:, h, :, h, :] for h in range(N_HEADS)], axis=2)
    return att.reshape(batch * t_new, d_att).astype(BF16)


def _mix_kernel(x_ref, hs_ref, att_ref, gl_ref, ga_ref, wl_ref, wa_ref, wm_ref, gc_ref, wcq_ref, x1_ref, qc_ref):
    lru = jnp.dot(hs_ref[...].astype(BF16), wl_ref[...], preferred_element_type=F32)
    att = jnp.dot(att_ref[...], wa_ref[...], preferred_element_type=F32)
    mixed = gl_ref[...] * lru + ga_ref[...] * att
    x1 = x_ref[...] + jnp.dot(mixed.astype(BF16), wm_ref[...], preferred_element_type=F32)
    x1_ref[...] = x1
    qc_ref[...] = jnp.dot(_rms(x1, gc_ref[...]).astype(BF16), wcq_ref[...], preferred_element_type=F32)


def _mix(x2d, hs, att, gl, ga, w_lru_out, w_att_out, w_mix_out, g_cross, w_cq, tm):
    n, d_model = x2d.shape
    tm = min(tm, n)
    assert n % tm == 0
    d_cross = w_cq.shape[1]
    row_spec = lambda w: pl.BlockSpec((tm, w), lambda i: (i, 0))
    return pl.pallas_call(
        _mix_kernel,
        grid=(n // tm,),
        in_specs=[row_spec(d_model), row_spec(hs.shape[1]), row_spec(att.shape[1]), row_spec(d_model), row_spec(d_model),
                  _const_spec(w_lru_out.shape), _const_spec(w_att_out.shape), _const_spec(w_mix_out.shape),
                  _const_spec((1, d_model)), _const_spec(w_cq.shape)],
        out_specs=(row_spec(d_model), row_spec(d_cross)),
        out_shape=(jax.ShapeDtypeStruct((n, d_model), F32), jax.ShapeDtypeStruct((n, d_cross), F32)),
        compiler_params=_cparams(("parallel",)),
        name="mix_out",
    )(x2d, hs, att, gl, ga, w_lru_out.astype(BF16), w_att_out.astype(BF16), w_mix_out.astype(BF16),
      g_cross.reshape(1, d_model), w_cq.astype(BF16))


def _mem_kv_kernel(mem_ref, g_ref, wk_ref, wv_ref, mk_ref, mv_ref, mkb_ref, mvb_ref):
    m = _rms(mem_ref[0], g_ref[...]).astype(BF16)
    mk = jnp.dot(m, wk_ref[...], preferred_element_type=F32)
    mv = jnp.dot(m, wv_ref[...], preferred_element_type=F32)
    hd = mk_ref.shape[-1]
    for h in range(C_HEADS):
        mk_ref[0, :, h, :] = mk[:, h * hd:(h + 1) * hd]
        mv_ref[0, :, h, :] = mv[:, h * hd:(h + 1) * hd]
    mkb_ref[0] = mk.astype(BF16)
    mvb_ref[0] = mv.astype(BF16)


def _mem_kv(mem, g_mem, w_mk, w_mv):
    b, s, d_model = mem.shape
    d_cross = w_mk.shape[1]
    hd = d_cross // C_HEADS
    kv_spec = pl.BlockSpec((1, s, C_HEADS, hd), lambda i: (i, 0, 0, 0))
    kv_shape = jax.ShapeDtypeStruct((b, s, C_HEADS, hd), F32)
    flat_spec = pl.BlockSpec((1, s, d_cross), lambda i: (i, 0, 0))
    flat_shape = jax.ShapeDtypeStruct((b, s, d_cross), BF16)
    return pl.pallas_call(
        _mem_kv_kernel,
        grid=(b,),
        in_specs=[pl.BlockSpec((1, s, d_model), lambda i: (i, 0, 0)), _const_spec((1, d_model)),
                  _const_spec(w_mk.shape), _const_spec(w_mv.shape)],
        out_specs=(kv_spec, kv_spec, flat_spec, flat_spec),
        out_shape=(kv_shape, kv_shape, flat_shape, flat_shape),
        compiler_params=_cparams(("parallel",)),
        name="mem_kv",
    )(mem, g_mem.reshape(1, d_model), w_mk.astype(BF16), w_mv.astype(BF16))


ROUTE_EXPERT0, ROUTE_EXPERT1, ROUTE_GATE0, ROUTE_GATE1 = 0, 1, 2, 3


def _lane_min_where(cond, lane):
    return jnp.min(jnp.where(cond, lane, LANES), axis=1, keepdims=True)


def _cross_kernel(x1_ref, qc_ref, mk_ref, mv_ref, wco_ref, gf_ref, wr_ref, br_ref, x2_ref, u3_ref, route_ref):
    n_seq = mk_ref.shape[0]
    tq = x1_ref.shape[0] // n_seq
    hd = mk_ref.shape[-1] // C_HEADS
    scale = hd ** -0.5
    seq_outs = []
    for sq in range(n_seq):
        rows = slice(sq * tq, (sq + 1) * tq)
        outs = []
        for h in range(C_HEADS):
            cols = slice(h * hd, (h + 1) * hd)
            s = _dot_nt(qc_ref[rows, cols].astype(BF16), mk_ref[sq, :, cols]) * scale
            pe = jnp.exp(s - jnp.max(s, axis=1, keepdims=True))
            o = jnp.dot(pe.astype(BF16), mv_ref[sq, :, cols], preferred_element_type=F32)
            outs.append(o / jnp.sum(pe, axis=1, keepdims=True))
        seq_outs.append(jnp.concatenate(outs, axis=1))
    o = jnp.concatenate(seq_outs, axis=0).astype(BF16)
    x2 = x1_ref[...] + jnp.dot(o, wco_ref[...], preferred_element_type=F32)
    x2_ref[...] = x2
    u3 = _rms(x2, gf_ref[...])
    for s in range(u3_ref.shape[1]):
        u3_ref[:, s, :] = u3[:, s * LANES:(s + 1) * LANES]

    logits = jnp.dot(u3.astype(BF16), wr_ref[...], preferred_element_type=F32) + br_ref[...]
    lane = lax.broadcasted_iota(I32, logits.shape, 1)
    lg = jnp.where(lane < N_GROUPS, logits, -jnp.inf)
    g_max = jnp.max(lg, axis=1, keepdims=True)
    g_sel = _lane_min_where(lg == g_max, lane)
    p_grp = 1.0 / jnp.sum(jnp.exp(lg - g_max), axis=1, keepdims=True)
    e_lo = N_GROUPS + g_sel * EXPERTS_PER_GROUP
    le = jnp.where((lane >= e_lo) & (lane < e_lo + EXPERTS_PER_GROUP), logits, -jnp.inf)
    m1 = jnp.max(le, axis=1, keepdims=True)
    i1 = _lane_min_where(le == m1, lane)
    le2 = jnp.where(lane == i1, -jnp.inf, le)
    m2 = jnp.max(le2, axis=1, keepdims=True)
    i2 = _lane_min_where(le2 == m2, lane)
    e2 = jnp.exp(m2 - m1)
    inv = p_grp / (1.0 + e2)
    route = jnp.where(lane == ROUTE_EXPERT0, (i1 - N_GROUPS).astype(F32), 0.0)
    route = jnp.where(lane == ROUTE_EXPERT1, (i2 - N_GROUPS).astype(F32), route)
    route = jnp.where(lane == ROUTE_GATE0, inv, route)
    route_ref[...] = jnp.where(lane == ROUTE_GATE1, inv * e2, route)


def _pack_router(w_rg, b_rg, w_re, b_re):
    d_model = w_rg.shape[0]
    used = w_rg.shape[1] + w_re.shape[1]
    w = jnp.concatenate([w_rg, w_re, jnp.zeros((d_model, LANES - used), w_rg.dtype)], axis=1).astype(BF16)
    b = jnp.concatenate([b_rg, b_re, jnp.zeros((LANES - used,), b_rg.dtype)]).reshape(1, LANES)
    return w, b


def _cross(x1, qc, mk, mv, w_co, g_ffn, w_router, b_router, batch, t_len, tq, n_seq):
    n, d_model = x1.shape
    tq = min(tq, t_len)
    n_seq = min(n_seq, batch)
    assert t_len % tq == 0 and batch % n_seq == 0 and (n_seq == 1 or tq == t_len)
    nq = t_len // tq
    d_cross = qc.shape[1]
    rows = n_seq * tq
    row_spec = lambda w: pl.BlockSpec((rows, w), lambda b, j: (b * nq + j, 0))
    mem_spec = pl.BlockSpec((n_seq,) + mk.shape[1:], lambda b, j: (b, 0, 0))
    return pl.pallas_call(
        _cross_kernel,
        grid=(batch // n_seq, nq),
        in_specs=[row_spec(d_model), row_spec(d_cross), mem_spec, mem_spec, _const_spec(w_co.shape),
                  _const_spec((1, d_model)), _const_spec(w_router.shape), _const_spec((1, LANES))],
        out_specs=(row_spec(d_model), pl.BlockSpec((rows, d_model // LANES, LANES), lambda b, j: (b * nq + j, 0, 0)),
                   row_spec(LANES)),
        out_shape=(jax.ShapeDtypeStruct((n, d_model), F32), jax.ShapeDtypeStruct((n, d_model // LANES, LANES), F32),
                   jax.ShapeDtypeStruct((n, LANES), F32)),
        compiler_params=_cparams(("parallel", "parallel")),
        name="cross_router",
    )(x1, qc, mk, mv, w_co.astype(BF16), g_ffn.reshape(1, d_model), w_router, b_router)


DMA_PRIORITIES = 2


def _row_gather_start(src_hbm, idx_ref, dst, sem, n_rows):
    def issue(i, carry):
        for u in range(DMA_PRIORITIES):
            r = i * DMA_PRIORITIES + u
            pltpu.make_async_copy(src_hbm.at[pl.ds(idx_ref[0, 0, r], 1)], dst.at[pl.ds(r, 1)], sem).start(priority=u)
        return carry
    lax.fori_loop(0, n_rows // DMA_PRIORITIES, issue, 0)


def _row_gather_wait(src_hbm, dst, sem, n_rows):
    pltpu.make_async_copy(src_hbm.at[pl.ds(0, n_rows)], dst, sem).wait()


def _expert_kernel(be_ref, tok_ref, tok_next_ref, x_hbm, w1_ref, w3_ref, w2_ref, y_ref, xbuf, sems):
    del be_ref
    i = pl.program_id(0)
    nb = pl.num_programs(0)
    rows = xbuf.shape[1]
    slot = i % 2

    @pl.when(i == 0)
    def _():
        _row_gather_start(x_hbm, tok_ref, xbuf.at[0], sems.at[0], rows)

    @pl.when(i + 1 < nb)
    def _():
        _row_gather_start(x_hbm, tok_next_ref, xbuf.at[1 - slot], sems.at[1 - slot], rows)

    _row_gather_wait(x_hbm, xbuf.at[slot], sems.at[slot], rows)
    n_slab = xbuf.shape[2]
    x = jnp.concatenate([xbuf[slot, :, s, :] for s in range(n_slab)], axis=1).astype(BF16)
    a = jnp.dot(x, w1_ref[0], preferred_element_type=F32)
    b = jnp.dot(x, w3_ref[0], preferred_element_type=F32)
    h = (a * jax.nn.sigmoid(a) * b).astype(BF16)
    y = jnp.dot(h, w2_ref[0], preferred_element_type=F32)
    for s in range(n_slab):
        y_ref[:, s, :] = y[:, s * LANES:(s + 1) * LANES]


def _combine_kernel(s0_ref, s1_ref, s0n_ref, s1n_ref, yb_hbm, x2_ref, route_ref, gfin_ref, out_ref, ybuf, sems):
    i = pl.program_id(0)
    nb = pl.num_programs(0)
    rows = x2_ref.shape[0]
    slot = i % 2

    def start(a_ref, b_ref, s):
        _row_gather_start(yb_hbm, a_ref, ybuf.at[s, 0], sems.at[s, 0], rows)
        _row_gather_start(yb_hbm, b_ref, ybuf.at[s, 1], sems.at[s, 1], rows)

    @pl.when(i == 0)
    def _():
        start(s0_ref, s1_ref, 0)

    @pl.when(i + 1 < nb)
    def _():
        start(s0n_ref, s1n_ref, 1 - slot)

    _row_gather_wait(yb_hbm, ybuf.at[slot, 0], sems.at[slot, 0], rows)
    _row_gather_wait(yb_hbm, ybuf.at[slot, 1], sems.at[slot, 1], rows)
    route = route_ref[...]
    g0 = route[:, ROUTE_GATE0:ROUTE_GATE0 + 1]
    g1 = route[:, ROUTE_GATE1:ROUTE_GATE1 + 1]
    y = jnp.concatenate([g0 * ybuf[slot, 0, :, s, :] + g1 * ybuf[slot, 1, :, s, :] for s in range(ybuf.shape[3])],
                        axis=1)
    out_ref[...] = _rms(x2_ref[...] + y, gfin_ref[...])


def _moe_and_final(x2, u3, route, w_e1, w_e3, w_e2, g_final, tm, blk):
    n, d_model = x2.shape
    n_slab = d_model // LANES
    n_exp = w_e1.shape[0]
    a_total = n * EXPERT_TOPK
    n_blocks = -(-(a_total + n_exp * (blk - 1)) // blk)
    p_rows = n_blocks * blk

    fe = route[:, ROUTE_EXPERT0:ROUTE_EXPERT1 + 1].astype(I32).reshape(-1)
    onehot = (fe[:, None] == jnp.arange(n_exp, dtype=I32)[None, :]).astype(I32)
    csum = jnp.cumsum(onehot, axis=0)
    rank = jnp.sum((csum - onehot) * onehot, axis=1)
    counts = csum[-1]
    padded = (counts + blk - 1) // blk * blk
    pad_end = jnp.cumsum(padded)
    dest = (pad_end - padded)[fe] + rank
    ft = jnp.arange(a_total, dtype=I32) // EXPERT_TOPK
    slot_tok = jnp.zeros((p_rows,), I32).at[dest].set(ft)
    block_start = jnp.arange(n_blocks, dtype=I32) * blk
    block_exp = jnp.minimum(jnp.sum((pad_end[None, :] <= block_start[:, None]).astype(I32), axis=1), n_exp - 1)

    tok3 = slot_tok.reshape(n_blocks, 1, blk)
    smem_blk = lambda w, f: pl.BlockSpec((1, 1, w), f, memory_space=pltpu.SMEM)
    w_spec = lambda shp: pl.BlockSpec((1,) + shp[1:], lambda i, be: (be[i], 0, 0))
    yb = pl.pallas_call(
        _expert_kernel,
        grid_spec=pltpu.PrefetchScalarGridSpec(
            num_scalar_prefetch=1,
            grid=(n_blocks,),
            in_specs=[smem_blk(blk, lambda i, be: (i, 0, 0)),
                      smem_blk(blk, lambda i, be: (jnp.minimum(i + 1, n_blocks - 1), 0, 0)),
                      pl.BlockSpec(memory_space=pl.ANY),
                      w_spec(w_e1.shape), w_spec(w_e3.shape), w_spec(w_e2.shape)],
            out_specs=pl.BlockSpec((blk, n_slab, LANES), lambda i, be: (i, 0, 0)),
            scratch_shapes=[pltpu.VMEM((2, blk, n_slab, LANES), F32), pltpu.SemaphoreType.DMA((2,))],
        ),
        out_shape=jax.ShapeDtypeStruct((p_rows, n_slab, LANES), F32),
        compiler_params=_cparams(("arbitrary",)),
        name="moe_experts",
    )(block_exp, tok3, tok3, u3, w_e1.astype(BF16), w_e3.astype(BF16), w_e2.astype(BF16))

    tm = min(tm, n)
    assert n % tm == 0
    nt = n // tm
    d2 = dest.reshape(n, EXPERT_TOPK)
    s0 = d2[:, 0].reshape(nt, 1, tm)
    s1 = d2[:, 1].reshape(nt, 1, tm)
    cur = lambda i: (i, 0, 0)
    nxt = lambda i: (jnp.minimum(i + 1, nt - 1), 0, 0)
    smem_blk2 = lambda f: pl.BlockSpec((1, 1, tm), f, memory_space=pltpu.SMEM)
    row_spec = lambda w: pl.BlockSpec((tm, w), lambda i: (i, 0))
    return pl.pallas_call(
        _combine_kernel,
        grid=(nt,),
        in_specs=[smem_blk2(cur), smem_blk2(cur), smem_blk2(nxt), smem_blk2(nxt),
                  pl.BlockSpec(memory_space=pl.ANY), row_spec(d_model), row_spec(LANES), _const_spec((1, d_model))],
        out_specs=row_spec(d_model),
        out_shape=jax.ShapeDtypeStruct((n, d_model), F32),
        scratch_shapes=[pltpu.VMEM((2, EXPERT_TOPK, tm, n_slab, LANES), F32),
                        pltpu.SemaphoreType.DMA((2, EXPERT_TOPK))],
        compiler_params=_cparams(("arbitrary",)),
        name="moe_combine",
    )(s0, s1, s0, s1, yb, x2, route, g_final.reshape(1, d_model))


def _layer(x, pos, conv_buf, h0, attend, mk, mv, lw, g_final, tiles):
    batch, t_len, d_model = x.shape
    n = batch * t_len
    d_lru = lw['w_conv'].shape[1]
    d_att = N_HEADS * HEAD_DIM
    d_idx = IDX_HEADS * IDX_DIM
    x2d = x.reshape(n, d_model)
    xl, q, k, v, qi, kiwi, gl, ga = _in_proj(x2d, lw['g_mix'], lw['w_in_packed'], pos, t_len, d_lru, d_att, d_idx,
                                             tiles['in_proj'])
    att = attend(qi, q, kiwi, k, v)
    hs, h_last, new_buf = _rglru(xl.reshape(batch, t_len, d_lru), conv_buf, h0, lw['w_conv'], lw['b_conv'],
                                 lw['w_ra'], lw['b_ra'], lw['w_ri'], lw['b_ri'], lw['lru_lambda'],
                                 tiles['lru_t'], tiles['lru_b'])
    x1, qc = _mix(x2d, hs.reshape(n, d_lru), att, gl, ga, lw['w_lru_out'], lw['w_att_out'], lw['w_mix_out'],
                  lw['g_cross'], lw['w_cq'], tiles['mix'])
    x2, u3, route = _cross(x1, qc, mk, mv, lw['w_co'], lw['g_ffn'], lw['w_router'], lw['b_router'], batch, t_len,
                           tiles['cross'], tiles['cross_seqs'])
    y = _moe_and_final(x2, u3, route, lw['w_e1'], lw['w_e3'], lw['w_e2'], g_final, tiles['combine'],
                       tiles['moe_rows'])
    k5 = k.reshape(batch, t_len, N_HEADS, HEAD_DIM)
    v5 = v.reshape(batch, t_len, N_HEADS, HEAD_DIM)
    ki = kiwi[:, :IDX_DIM].reshape(batch, t_len, IDX_DIM)
    return y.reshape(batch, t_len, d_model), new_buf, h_last, k5, v5, ki


def kernel(x_prompt, mem_prompt, x_sample, cache_k, cache_v, cache_kidx, cache_mem_k, cache_mem_v, state_conv, state_lru, page_table, g_mix, w_in, w_conv, b_conv, w_ra, b_ra, w_ri, b_ri, lru_lambda, w_lru_out, w_att_out, w_mix_out, g_cross, g_mem, w_cq, w_mk, w_mv, w_co, g_ffn, w_rg, b_rg, w_re, b_re, w_e1, w_e3, w_e2, g_final):
    depth = w_in.shape[0]
    assert depth == 1, "the final norm is fused into the last layer's MoE combine; one layer supported"
    l = 0
    b_p, t_p, _ = x_prompt.shape
    b_s, t_s, _ = x_sample.shape
    past = page_table.shape[1] * cache_k.shape[2]
    d_lru = w_conv.shape[2]
    d_att = N_HEADS * HEAD_DIM
    d_idx = IDX_HEADS * IDX_DIM
    w_router, b_router = _pack_router(w_rg[l], b_rg[l], w_re[l], b_re[l])
    lw = {
        'g_mix': g_mix[l], 'w_in_packed': _pack_w_in(w_in[l], d_lru, d_att, d_idx),
        'w_conv': w_conv[l], 'b_conv': b_conv[l], 'w_ra': w_ra[l], 'b_ra': b_ra[l], 'w_ri': w_ri[l], 'b_ri': b_ri[l],
        'lru_lambda': lru_lambda[l], 'w_lru_out': w_lru_out[l], 'w_att_out': w_att_out[l], 'w_mix_out': w_mix_out[l],
        'g_cross': g_cross[l], 'w_cq': w_cq[l], 'w_co': w_co[l], 'g_ffn': g_ffn[l],
        'w_router': w_router, 'b_router': b_router, 'w_e1': w_e1[l], 'w_e3': w_e3[l], 'w_e2': w_e2[l],
    }

    mk_p, mv_p, mkb_p, mvb_p = _mem_kv(mem_prompt, g_mem[l], w_mk[l], w_mv[l])
    attend_p = functools.partial(_prompt_attention, batch=b_p, t_len=t_p)
    tiles_p = {'in_proj': 512, 'lru_t': 128, 'lru_b': 8, 'mix': 512, 'cross': 512, 'cross_seqs': 1,
               'combine': 256, 'moe_rows': 256}
    y_p, buf_p, h_p, k_p, v_p, ki_p = _layer(
        x_prompt, jnp.arange(t_p), jnp.zeros((b_p, CONV_W - 1, d_lru), F32), jnp.zeros((b_p, d_lru), F32),
        attend_p, mkb_p, mvb_p, lw, g_final, tiles_p)

    def attend_s(qi, q, kiwi, k, v):
        return _sample_attention(qi, q, kiwi, k, v, cache_k, cache_v, cache_kidx, page_table, l, t_s,
                                 pg_sel=32, pg_att=16)
    tiles_s = {'in_proj': 512, 'lru_t': 8, 'lru_b': 8, 'mix': 512, 'cross': t_s, 'cross_seqs': 16,
               'combine': 256, 'moe_rows': 128}
    flat_mem = lambda m: m.reshape(m.shape[0], m.shape[1], -1).astype(BF16)
    y_s, buf_s, h_s, k_s, v_s, ki_s = _layer(
        x_sample, past + jnp.arange(t_s), state_conv[l], state_lru[l], attend_s,
        flat_mem(cache_mem_k[l]), flat_mem(cache_mem_v[l]), lw, g_final, tiles_s)

    st = lambda a: a[None]
    return (y_p, y_s, st(k_p), st(v_p), st(ki_p), st(buf_p), st(h_p), st(mk_p), st(mv_p),
            st(k_s), st(v_s), st(ki_s), st(buf_s), st(h_s))
```
